```python
import math
import jax, jax.numpy as jnp
from jax import lax
import numpy as np

D_MODEL = 1024
BATCH = 8
SEQ = 2048
DEPTH = 4

CTX_LEN = 256
GRID_W = 64
HEAD_DIM = 64
MIX_W = D_MODEL
N_A = (MIX_W // 2) // (2 * HEAD_DIM)
D_A = HEAD_DIM
N_B = (MIX_W // 4) // HEAD_DIM
D_BK = HEAD_DIM
D_BV = HEAD_DIM
RET_CHUNK = 128
N_C = (MIX_W // 4) // HEAD_DIM
N_C_KV = 2
C_GROUP = N_C // N_C_KV
D_C = HEAD_DIM
WINDOW = 128
Q_BLOCK = 128
D_FF = 4 * D_MODEL
ROPE_BASE = 10000.0
NORM_EPS = 1e-6
ADA_INIT = 0.5

PROJ_SIZES = (N_A * 2 * D_A, N_A * 2 * D_A, N_A * 2 * D_A,
              N_B * D_BK, N_B * D_BK, N_B * D_BV, N_B * D_BV,
              N_C * D_C, N_C_KV * D_C, N_C_KV * D_C)
D_PROJ = sum(PROJ_SIZES)
D_MIX_OUT = N_A * 2 * D_A + N_B * D_BV + N_C * D_C

kernel_name = "hybrid_parallel_mixer_dit"


def rms_norm(x, g):
    xf = x.astype(jnp.float32)
    y = xf * lax.rsqrt(jnp.mean(xf * xf, axis=-1, keepdims=True) + NORM_EPS)
    return (y * g.astype(jnp.float32)).astype(x.dtype)


def modulate(x, g, shift, scale):
    return rms_norm(x, g) * (1 + scale) + shift


def split_proj(p):
    idx = np.cumsum(PROJ_SIZES)[:-1].tolist()
    return jnp.split(p, idx, axis=-1)


def axial_rope_tables(T, d):
    rows = T // GRID_W
    r = jnp.repeat(jnp.arange(rows, dtype=jnp.float32), GRID_W)
    col = jnp.tile(jnp.arange(GRID_W, dtype=jnp.float32), rows)
    nf = d // 4
    inv = ROPE_BASE ** (-jnp.arange(nf, dtype=jnp.float32) / nf)
    ar = r[:, None] * inv
    ac = col[:, None] * inv
    ang = jnp.concatenate([ar, ar, ac, ac], axis=-1)
    return jnp.cos(ang), jnp.sin(ang)


def apply_rope(x, cos, sin):
    x1, x2, x3, x4 = jnp.split(x, 4, axis=-1)
    rot = jnp.concatenate([-x2, x1, -x4, x3], axis=-1)
    shp = (1, x.shape[1]) + (1,) * (x.ndim - 3) + (x.shape[-1],)
    return (x * cos.reshape(shp) + rot * sin.reshape(shp)).astype(x.dtype)


def diff_attn_blocks(q, k, v, lam):
    B, Tq, H, _, d = q.shape
    nb = Tq // Q_BLOCK
    qb = q.reshape(B, nb, Q_BLOCK, H, 2, d).swapaxes(0, 1)
    scale = d ** -0.5

    def one(qblk):
        s = jnp.einsum('bqhcd,bkhcd->bhcqk', qblk, k).astype(jnp.float32) * scale
        p = jax.nn.softmax(s, axis=-1)
        w = p[:, :, 0] - lam * p[:, :, 1]
        return jnp.einsum('bhqk,bkhe->bqhe', w.astype(v.dtype), v)

    o = lax.map(one, qb)
    return o.swapaxes(0, 1).reshape(B, Tq, H, -1)


def diff_attention_mixer(px, pc, lq1, lk1, lq2, lk2, subln_g, lam_init, cos, sin, need_ctx):
    qx, kx, vx = px
    qc, kc, vc = pc
    B, T, _ = qx.shape
    L = qc.shape[1]
    qx = apply_rope(qx.reshape(B, T, N_A, 2, D_A), cos, sin)
    kx = apply_rope(kx.reshape(B, T, N_A, 2, D_A), cos, sin)
    vx = vx.reshape(B, T, N_A, 2 * D_A)
    qc = qc.reshape(B, L, N_A, 2, D_A)
    kc = kc.reshape(B, L, N_A, 2, D_A)
    vc = vc.reshape(B, L, N_A, 2 * D_A)
    f32 = jnp.float32
    lam = (jnp.exp(jnp.sum(lq1.astype(f32) * lk1.astype(f32)))
           - jnp.exp(jnp.sum(lq2.astype(f32) * lk2.astype(f32))) + lam_init)

    def finish(o):
        o = rms_norm(o, subln_g) * (1.0 - lam_init)
        return o.reshape(o.shape[0], o.shape[1], -1)

    k_all = jnp.concatenate([kx, kc], axis=1)
    v_all = jnp.concatenate([vx, vc], axis=1)
    out_x = finish(diff_attn_blocks(qx, k_all, v_all, lam))
    out_c = finish(diff_attn_blocks(qc, kc, vc, lam)) if need_ctx else None
    return out_x, out_c


def retention_chunks(q, k, v, log_gamma, state0, inclusive):
    B, H, T, dk = q.shape
    dv = v.shape[-1]
    C = RET_CHUNK
    n = T // C

    def chunks(a):
        return a.reshape(B, H, n, C, a.shape[-1]).transpose(2, 0, 1, 3, 4)

    i = jnp.arange(C, dtype=jnp.float32)
    diff = i[:, None] - i[None, :]
    mask = (diff >= 0) if inclusive else (diff > 0)
    intra = jnp.where(mask, jnp.exp(jnp.where(mask, diff, 0.0)[None] * log_gamma[:, None, None]), 0.0)
    q_dec = jnp.exp((i + 1.0)[None] * log_gamma[:, None])
    k_dec = jnp.exp((C - 1.0 - i)[None] * log_gamma[:, None])
    c_dec = jnp.exp(C * log_gamma)[:, None, None]

    def step(state, qkv):
        qj, kj, vj = qkv
        s = jnp.einsum('bhid,bhmd->bhim', qj, kj) * intra
        o = (jnp.einsum('bhim,bhme->bhie', s, vj)
             + jnp.einsum('bhid,bhde->bhie', qj * q_dec[..., None], state))
        state = c_dec * state + jnp.einsum('bhmd,bhme->bhde', kj * k_dec[..., None], vj)
        return state, o

    _, o = lax.scan(step, state0, (chunks(q), chunks(k), chunks(v)))
    return o.transpose(1, 2, 0, 3, 4).reshape(B, H, T, dv)


def gated_group_norm(y, g):
    mu = jnp.mean(y, axis=-1, keepdims=True)
    yc = y - mu
    y = yc * lax.rsqrt(jnp.mean(yc * yc, axis=-1, keepdims=True) + NORM_EPS)
    B, H, T, dv = y.shape
    y = y.transpose(0, 2, 1, 3).reshape(B, T, H * dv)
    return (jax.nn.silu(g.astype(jnp.float32)) * y).astype(g.dtype)


def retention_mixer(px, pc, decay_f, decay_b, cos, sin, need_ctx):
    qx, kx, vx, gx = px
    qc, kc, vc, gc = pc
    B, T, _ = qx.shape
    L = qc.shape[1]
    f32 = jnp.float32
    kscale = D_BK ** -0.5

    def bhtd(a, d):
        return a.reshape(a.shape[0], a.shape[1], N_B, d).astype(f32).transpose(0, 2, 1, 3)

    qx_ = apply_rope(qx.reshape(B, T, N_B, D_BK), cos, sin).astype(f32).transpose(0, 2, 1, 3)
    kx_ = (apply_rope(kx.reshape(B, T, N_B, D_BK), cos, sin).astype(f32) * kscale).transpose(0, 2, 1, 3)
    vx_ = bhtd(vx, D_BV)
    qc_ = bhtd(qc, D_BK)
    kc_ = bhtd(kc, D_BK) * kscale
    vc_ = bhtd(vc, D_BV)
    lg_f = jax.nn.log_sigmoid(decay_f.astype(f32))
    lg_b = jax.nn.log_sigmoid(decay_b.astype(f32))

    pos = jnp.arange(L, dtype=f32)
    s_f = jnp.einsum('bhld,bhle,hl->bhde', kc_, vc_, jnp.exp((L - 1.0 - pos)[None] * lg_f[:, None]))
    s_b = jnp.einsum('bhld,bhle,hl->bhde', kc_, vc_, jnp.exp(pos[None] * lg_b[:, None]))

    def flip(a):
        return a[:, :, ::-1]

    of = retention_chunks(qx_, kx_, vx_, lg_f, s_f, True)
    ob = flip(retention_chunks(flip(qx_), flip(kx_), flip(vx_), lg_b, s_b, False))
    out_x = gated_group_norm(of + ob, gx)
    out_c = None
    if need_ctx:
        z = jnp.zeros((B, N_B, D_BK, D_BV), f32)
        ocf = retention_chunks(qc_, kc_, vc_, lg_f, z, True)
        ocb = flip(retention_chunks(flip(qc_), flip(kc_), flip(vc_), lg_b, z, False))
        out_c = gated_group_norm(ocf + ocb, gc)
    return out_x, out_c


def sink_softmax(s, sink):
    m = jnp.maximum(jnp.max(s, axis=-1, keepdims=True), sink)
    p = jnp.exp(s - m)
    return p / (jnp.sum(p, axis=-1, keepdims=True) + jnp.exp(sink - m))


def window_gqa_mixer(px, pc, sink, cos, sin, need_ctx):
    qx, kx, vx = px
    qc, kc, vc = pc
    B, T, _ = qx.shape
    L = qc.shape[1]
    WB = WINDOW
    nb = T // WB
    scale = D_C ** -0.5
    sink_b = sink.astype(jnp.float32).reshape(N_C_KV, C_GROUP, 1, 1)
    qx = apply_rope(qx.reshape(B, T, N_C_KV, C_GROUP, D_C), cos, sin)
    kx = apply_rope(kx.reshape(B, T, N_C_KV, D_C), cos, sin)
    vx = vx.reshape(B, T, N_C_KV, D_C)
    qc = qc.reshape(B, L, N_C_KV, C_GROUP, D_C)
    kc = kc.reshape(B, L, N_C_KV, D_C)
    vc = vc.reshape(B, L, N_C_KV, D_C)

    qb = qx.reshape(B, nb, WB, N_C_KV, C_GROUP, D_C)

    def band(a):
        ab = jnp.pad(a.reshape(B, nb, WB, N_C_KV, D_C), ((0, 0), (1, 1), (0, 0), (0, 0), (0, 0)))
        return jnp.concatenate([ab[:, :-2], ab[:, 1:-1], ab[:, 2:]], axis=2)

    kwin = band(kx)
    vwin = band(vx)
    blk = jnp.arange(nb)[:, None, None]
    qpos = blk * WB + jnp.arange(WB)[None, :, None]
    kpos = (blk - 1) * WB + jnp.arange(3 * WB)[None, None, :]
    valid = (kpos >= 0) & (kpos < T) & (jnp.abs(kpos - qpos) <= WINDOW)
    s_loc = jnp.einsum('bnqhgd,bnkhd->bnhgqk', qb, kwin).astype(jnp.float32) * scale
    s_loc = jnp.where(valid[None, :, None, None], s_loc, -jnp.inf)
    s_ctx = jnp.einsum('bnqhgd,bchd->bnhgqc', qb, kc).astype(jnp.float32) * scale
    p = sink_softmax(jnp.concatenate([s_loc, s_ctx], axis=-1), sink_b)
    p = p.astype(vx.dtype)
    o = (jnp.einsum('bnhgqk,bnkhd->bnqhgd', p[..., :3 * WB], vwin)
         + jnp.einsum('bnhgqc,bchd->bnqhgd', p[..., 3 * WB:], vc))
    out_x = o.reshape(B, T, N_C * D_C)
    out_c = None
    if need_ctx:
        s = jnp.einsum('bqhgd,bkhd->bhgqk', qc, kc).astype(jnp.float32) * scale
        pc_ = sink_softmax(s, sink_b).astype(vc.dtype)
        out_c = jnp.einsum('bhgqk,bkhd->bqhgd', pc_, vc).reshape(B, L, N_C * D_C)
    return out_x, out_c


def sq_relu_mlp(h, w1, w2):
    a = jax.nn.relu(h @ w1)
    return (a * a) @ w2


def setup_inputs(seed: int = 0) -> dict:
    key = jax.random.key(seed)
    ks = jax.random.split(key, 24)
    f32 = jnp.float32

    def nrm(k, shape, scale):
        return jax.random.normal(k, shape, f32) * scale

    ret_base = jnp.log(2.0 ** (5.0 + jnp.arange(N_B, dtype=f32)) - 1.0)
    return {
        "x": nrm(ks[0], (BATCH, SEQ, D_MODEL), 1.0),
        "c": nrm(ks[1], (BATCH, D_MODEL), 1.0),
        "ctx": nrm(ks[2], (BATCH, CTX_LEN, D_MODEL), 1.0),
        "c_ctx": nrm(ks[3], (D_MODEL,), 1.0),
        "w_ada": nrm(ks[4], (DEPTH, D_MODEL, 6 * D_MODEL), ADA_INIT * D_MODEL ** -0.5),
        "b_ada": nrm(ks[5], (DEPTH, 6 * D_MODEL), 0.01),
        "g_mix": 1.0 + nrm(ks[6], (DEPTH, D_MODEL), 0.05),
        "g_mlp": 1.0 + nrm(ks[7], (DEPTH, D_MODEL), 0.05),
        "w_in": nrm(ks[8], (DEPTH, D_MODEL, D_PROJ), D_MODEL ** -0.5),
        "w_out": nrm(ks[9], (DEPTH, D_MIX_OUT, D_MODEL), D_MIX_OUT ** -0.5),
        "lam_q1": nrm(ks[10], (DEPTH, D_A), 0.1),
        "lam_k1": nrm(ks[11], (DEPTH, D_A), 0.1),
        "lam_q2": nrm(ks[12], (DEPTH, D_A), 0.1),
        "lam_k2": nrm(ks[13], (DEPTH, D_A), 0.1),
        "subln_g": 1.0 + nrm(ks[14], (DEPTH, 2 * D_A), 0.05),
        "ret_decay_fwd": ret_base[None] + nrm(ks[15], (DEPTH, N_B), 0.1),
        "ret_decay_bwd": ret_base[None] + nrm(ks[16], (DEPTH, N_B), 0.1),
        "sink_logit": nrm(ks[17], (DEPTH, N_C), 0.5),
        "w_mlp1": nrm(ks[18], (DEPTH, D_MODEL, D_FF), D_MODEL ** -0.5),
        "w_mlp2": nrm(ks[19], (DEPTH, D_FF, D_MODEL), D_FF ** -0.5),
        "g_final": 1.0 + nrm(ks[20], (D_MODEL,), 0.05),
    }


def reference(x, c, ctx, c_ctx, w_ada, b_ada, g_mix, g_mlp, w_in, w_out, lam_q1, lam_k1, lam_q2,
              lam_k2, subln_g, ret_decay_fwd, ret_decay_bwd, sink_logit, w_mlp1, w_mlp2, g_final):
    T = x.shape[1]
    cos, sin = axial_rope_tables(T, HEAD_DIM)
    silu_c = jax.nn.silu(c)
    silu_cc = jax.nn.silu(c_ctx)[None]
    for l in range(DEPTH):
        need_ctx = l < DEPTH - 1
        mx = (silu_c @ w_ada[l] + b_ada[l])[:, None, :]
        mc = (silu_cc @ w_ada[l] + b_ada[l])[:, None, :]
        sh1, sc1, gt1, sh2, sc2, gt2 = jnp.split(mx, 6, axis=-1)
        csh1, csc1, cgt1, csh2, csc2, cgt2 = jnp.split(mc, 6, axis=-1)
        hx = modulate(x, g_mix[l], sh1, sc1)
        hc = modulate(ctx, g_mix[l], csh1, csc1)
        px = split_proj(hx @ w_in[l])
        pc = split_proj(hc @ w_in[l])
        lam_init = 0.8 - 0.6 * math.exp(-0.3 * l)
        ax, ac = diff_attention_mixer(px[0:3], pc[0:3], lam_q1[l], lam_k1[l], lam_q2[l], lam_k2[l],
                                      subln_g[l], lam_init, cos, sin, need_ctx)
        bx, bc = retention_mixer(px[3:7], pc[3:7], ret_decay_fwd[l], ret_decay_bwd[l], cos, sin, need_ctx)
        cx, cc = window_gqa_mixer(px[7:10], pc[7:10], sink_logit[l], cos, sin, need_ctx)
        x = x + gt1 * (jnp.concatenate([ax, bx, cx], axis=-1) @ w_out[l])
        x = x + gt2 * sq_relu_mlp(modulate(x, g_mlp[l], sh2, sc2), w_mlp1[l], w_mlp2[l])
        if need_ctx:
            ctx = ctx + cgt1 * (jnp.concatenate([ac, bc, cc], axis=-1) @ w_out[l])
            ctx = ctx + cgt2 * sq_relu_mlp(modulate(ctx, g_mlp[l], csh2, csc2), w_mlp1[l], w_mlp2[l])
    return rms_norm(x, g_final)
```

```python
import functools
import math

import jax
import jax.numpy as jnp
from jax import lax
from jax.experimental import pallas as pl
from jax.experimental.pallas import tpu as pltpu

D_MODEL = 1024
HEAD_DIM = 64
LANES = 128
GRID_W = 64
N_A = 4
RET_CHUNK = 128
WINDOW = 128
D_FF = 4 * D_MODEL
ROPE_BASE = 10000.0
NORM_EPS = 1e-6
ROW_TILE = 512
Q_TILE = 256
VMEM_LIMIT = 56 * 1024 * 1024

PA_W = 3 * N_A * 2 * HEAD_DIM
PB_W = 4 * 4 * HEAD_DIM
PC_W = 6 * LANES

BF16 = jnp.bfloat16
F32 = jnp.float32


def _dot(a, b):
    return jnp.dot(a, b, preferred_element_type=F32)


def _dot_nt(a, b):
    return lax.dot_general(a, b, (((1,), (1,)), ((), ())), preferred_element_type=F32)


def _dot_tn(a, b):
    return lax.dot_general(a, b, (((0,), (0,)), ((), ())), preferred_element_type=F32)


def _lane_lo(shape):
    return lax.broadcasted_iota(jnp.int32, shape, len(shape) - 1) < HEAD_DIM


def _params(*sem):
    return pltpu.CompilerParams(dimension_semantics=sem, vmem_limit_bytes=VMEM_LIMIT)


def _const_spec(shape):
    nd = len(shape)
    return pl.BlockSpec(shape, lambda *_: (0,) * nd, pipeline_mode=pl.Buffered(1))


def _ada_kernel(cc_ref, w_ref, b_ref, o_ref):
    cc = cc_ref[...]
    a = (cc * jax.nn.sigmoid(cc)).astype(BF16)
    o_ref[0] = _dot(a, w_ref[0].astype(BF16)) + b_ref[0]


def _ada_params(cc, w_ada, b_ada):
    depth, d, n6 = w_ada.shape
    tn = n6 // 4
    rows = cc.shape[0]
    return pl.pallas_call(
        _ada_kernel,
        grid=(depth, n6 // tn),
        in_specs=[pl.BlockSpec((rows, d), lambda l, j: (0, 0)),
                  pl.BlockSpec((1, d, tn), lambda l, j: (l, 0, j)),
                  pl.BlockSpec((1, 1, tn), lambda l, j: (l, 0, j))],
        out_specs=pl.BlockSpec((1, rows, tn), lambda l, j: (l, 0, j)),
        out_shape=jax.ShapeDtypeStruct((depth, rows, n6), F32),
        compiler_params=_params("arbitrary", "arbitrary"),
        name="ada_params",
    )(cc, w_ada, b_ada.reshape(depth, 1, n6))


def _modulated_norm(x, g, shift, scale):
    ms = jnp.mean(x * x, axis=-1, keepdims=True)
    return (x * lax.rsqrt(ms + NORM_EPS) * g) * (1.0 + scale) + shift


def _rope(a, cos, sin_up, sin_dn):
    return (a * cos + pltpu.roll(a, LANES - 16, axis=1) * sin_up
            + pltpu.roll(a, 16, axis=1) * sin_dn)


_A_W = N_A * 2 * HEAD_DIM
_SEGS_A = ((0, 4, True, HEAD_DIM ** -0.5), (_A_W, 4, True, 1.0), (2 * _A_W, 4, False, 1.0))
_B0 = 3 * _A_W
_SEGS_B = ((_B0, 2, True, 1.0), (_B0 + 256, 2, True, HEAD_DIM ** -0.5),
           (_B0 + 512, 2, False, 1.0), (_B0 + 768, 2, False, 1.0))
_C0 = _B0 + 1024


def _in_proj_kernel(x_ref, mod_ref, g_ref, w_ref, cos_ref, sup_ref, sdn_ref, pa_ref, pb_ref, pc_ref):
    mod = mod_ref[0]
    h = _modulated_norm(x_ref[...], g_ref[...], mod[0:1], mod[1:2]).astype(BF16)
    cos, sup, sdn = cos_ref[...], sup_ref[...], sdn_ref[...]

    def project(col0, nblk):
        return _dot(h, w_ref[:, col0:col0 + nblk * LANES])

    def finish(acc, j, rope, mult):
        a = acc[:, j * LANES:(j + 1) * LANES]
        if rope:
            a = _rope(a, cos, sup, sdn)
        if mult != 1.0:
            a = a * mult
        return a

    for segs, ref in ((_SEGS_A, pa_ref), (_SEGS_B, pb_ref)):
        out_col = 0
        for col0, nblk, rope, mult in segs:
            acc = project(col0, nblk)
            for j in range(nblk):
                ref[:, out_col:out_col + LANES] = finish(acc, j, rope, mult).astype(ref.dtype)
                out_col += LANES

    acc = project(_C0, 4)
    lo = _lane_lo((acc.shape[0], LANES))
    for j in range(2):
        pc_ref[:, j * LANES:(j + 1) * LANES] = finish(acc, j, True, HEAD_DIM ** -0.5).astype(BF16)
    for j, rope in ((2, True), (3, False)):
        a = finish(acc, j, rope, 1.0)
        r = pltpu.roll(a, HEAD_DIM, axis=1)
        base = (2 + 2 * (j - 2)) * LANES
        pc_ref[:, base:base + LANES] = jnp.where(lo, a, r).astype(BF16)
        pc_ref[:, base + LANES:base + 2 * LANES] = jnp.where(lo, r, a).astype(BF16)


def _in_proj(xs, mods, g, w, cos, sup, sdn, n_lat_tiles, tiles_per_batch, n_ctx_mod):
    rows, d = xs.shape
    n_tiles = rows // ROW_TILE
    dproj = w.shape[1]

    def mod_idx(i):
        return (jnp.where(i < n_lat_tiles, i // tiles_per_batch, n_ctx_mod), 0, 0)

    def tab_idx(i):
        return (jnp.where(i < n_lat_tiles, i % tiles_per_batch, tiles_per_batch), 0)

    tab = pl.BlockSpec((ROW_TILE, LANES), tab_idx)
    return pl.pallas_call(
        _in_proj_kernel,
        grid=(n_tiles,),
        in_specs=[pl.BlockSpec((ROW_TILE, d), lambda i: (i, 0)),
                  pl.BlockSpec((1, 6, d), mod_idx),
                  _const_spec((1, d)),
                  _const_spec((d, dproj)),
                  tab, tab, tab],
        out_specs=[pl.BlockSpec((ROW_TILE, PA_W), lambda i: (i, 0)),
                   pl.BlockSpec((ROW_TILE, PB_W), lambda i: (i, 0)),
                   pl.BlockSpec((ROW_TILE, PC_W), lambda i: (i, 0))],
        out_shape=[jax.ShapeDtypeStruct((rows, PA_W), BF16),
                   jax.ShapeDtypeStruct((rows, PB_W), F32),
                   jax.ShapeDtypeStruct((rows, PC_W), BF16)],
        compiler_params=_params("arbitrary"),
        name="in_proj",
    )(xs, mods, g, w, cos, sup, sdn)


def _diff_attn_kernel(q_ref, kx_ref, vx_ref, kc_ref, vc_ref, lam_ref, g_ref, o_ref, *, lam_init,
                      n_lat_q):
    qt = pl.program_id(2)
    q = q_ref[...]
    lo = _lane_lo(q.shape)
    zero = jnp.zeros_like(q)
    q0 = jnp.where(lo, q, zero)
    q1 = jnp.where(lo, zero, q)
    lv = lam_ref[...]
    lam = (jnp.exp(jnp.sum(lv[0:1] * lv[1:2], axis=-1, keepdims=True))
           - jnp.exp(jnp.sum(lv[2:3] * lv[3:4], axis=-1, keepdims=True)) + lam_init)

    def softmax_parts(qc, ks):
        s = [_dot_nt(qc, k) for k in ks]
        m = s[0].max(axis=-1, keepdims=True)
        for t in s[1:]:
            m = jnp.maximum(m, t.max(axis=-1, keepdims=True))
        p = [jnp.exp(t - m) for t in s]
        den = p[0].sum(axis=-1, keepdims=True)
        for t in p[1:]:
            den = den + t.sum(axis=-1, keepdims=True)
        return p, 1.0 / den

    def attend(ks, vs):
        p0, r0 = softmax_parts(q0, ks)
        p1, r1 = softmax_parts(q1, ks)
        r1 = r1 * lam
        o = None
        for a, b, v in zip(p0, p1, vs):
            w = (a * r0 - b * r1).astype(BF16)
            t = _dot(w, v)
            o = t if o is None else o + t
        ms = jnp.mean(o * o, axis=-1, keepdims=True)
        o = o * lax.rsqrt(ms + NORM_EPS) * g_ref[...] * (1.0 - lam_init)
        o_ref[...] = o.astype(o_ref.dtype)

    @pl.when(qt < n_lat_q)
    def _():
        attend([kx_ref[...], kc_ref[...]], [vx_ref[...], vc_ref[...]])

    @pl.when(qt >= n_lat_q)
    def _():
        attend([kc_ref[...]], [vc_ref[...]])


def _diff_attn(pa, lamv, g, lam_init, batch, seq, ctx_len, need_ctx):
    rows = pa.shape[0]
    n_lat_q = seq // Q_TILE
    ctx_q0 = batch * seq // Q_TILE
    ctx_k0 = batch * seq // ctx_len
    nq = n_lat_q + (1 if need_ctx else 0)

    def q_idx(b, h, t):
        return (jnp.where(t < n_lat_q, b * n_lat_q + t, ctx_q0 + b), h)

    return pl.pallas_call(
        functools.partial(_diff_attn_kernel, lam_init=lam_init, n_lat_q=n_lat_q),
        grid=(batch, N_A, nq),
        in_specs=[pl.BlockSpec((Q_TILE, LANES), q_idx),
                  pl.BlockSpec((seq, LANES), lambda b, h, t: (b, N_A + h)),
                  pl.BlockSpec((seq, LANES), lambda b, h, t: (b, 2 * N_A + h)),
                  pl.BlockSpec((ctx_len, LANES), lambda b, h, t: (ctx_k0 + b, N_A + h)),
                  pl.BlockSpec((ctx_len, LANES), lambda b, h, t: (ctx_k0 + b, 2 * N_A + h)),
                  _const_spec((4, HEAD_DIM)),
                  _const_spec((1, LANES))],
        out_specs=pl.BlockSpec((Q_TILE, LANES), q_idx),
        out_shape=jax.ShapeDtypeStruct((rows, N_A * LANES), BF16),
        compiler_params=_params("arbitrary", "arbitrary", "arbitrary"),
        name="diff_attn",
    )(pa, pa, pa, pa, pa, lamv, g)


def _retention_kernel(*refs, n_chunks, with_ctx_state):
    if with_ctx_state:
        q_ref, k_ref, v_ref, g_ref, kc_ref, vc_ref, df_ref, db_ref, o_ref, acc_ref = refs
    else:
        q_ref, k_ref, v_ref, g_ref, df_ref, db_ref, o_ref, acc_ref = refs
    C = RET_CHUNK
    lgf = jax.nn.log_sigmoid(df_ref[0])
    lgb = jax.nn.log_sigmoid(db_ref[0])
    row = lax.broadcasted_iota(jnp.int32, (C, LANES), 0).astype(F32)
    qdec_f = jnp.exp((row + 1.0) * lgf)
    kdec_f = jnp.exp((C - 1.0 - row) * lgf)
    cdec_f = jnp.exp(C * lgf)
    qdec_b = jnp.exp((C - row) * lgb)
    kdec_b = jnp.exp(row * lgb)
    cdec_b = jnp.exp(C * lgb)
    ii = lax.broadcasted_iota(jnp.int32, (C, C), 0)
    mm = lax.broadcasted_iota(jnp.int32, (C, C), 1)
    dist = (ii - mm).astype(F32)
    causal = ii >= mm
    fwd_d = jnp.where(causal, dist, 0.0)
    bwd_d = jnp.where(causal, 0.0, -dist)
    intra = [jnp.where(causal, jnp.exp(fwd_d * lgf[:, c:c + 1]), jnp.exp(bwd_d * lgb[:, c:c + 1]))
             for c in (0, HEAD_DIM)]
    lo = _lane_lo((C, LANES))
    same_head = lo == (lax.broadcasted_iota(jnp.int32, (C, LANES), 0) < HEAD_DIM)

    def state_of(k, v):
        return jnp.where(same_head, _dot_tn(k.astype(BF16), v.astype(BF16)), 0.0)

    if with_ctx_state:
        n_ctx = kc_ref.shape[0]
        pos = lax.broadcasted_iota(jnp.int32, (n_ctx, LANES), 0).astype(F32)
        kc, vc = kc_ref[...], vc_ref[...]
        s_f0 = state_of(kc * jnp.exp((n_ctx - 1.0 - pos) * lgf), vc)
        s_b0 = state_of(kc * jnp.exp(pos * lgb), vc)
    else:
        s_f0 = jnp.zeros((LANES, LANES), F32)
        s_b0 = s_f0

    def chunk(j):
        return pl.ds(pl.multiple_of(j * C, C), C)

    def back(t, s_b):
        sl = chunk(n_chunks - 1 - t)
        q, k, v = q_ref[sl, :], k_ref[sl, :], v_ref[sl, :]
        acc_ref[sl, :] = _dot((q * qdec_b).astype(BF16), s_b.astype(BF16))
        return cdec_b * s_b + state_of(k * kdec_b, v)

    lax.fori_loop(0, n_chunks, back, s_b0)

    def fwd(j, s_f):
        sl = chunk(j)
        q, k, v, g = q_ref[sl, :], k_ref[sl, :], v_ref[sl, :], g_ref[sl, :]
        zero = jnp.zeros_like(q)
        kb, vb = k.astype(BF16), v.astype(BF16)
        y = acc_ref[sl, :] + _dot((q * qdec_f).astype(BF16), s_f.astype(BF16))
        inner = []
        for c in range(2):
            qh = jnp.where(lo, q, zero) if c == 0 else jnp.where(lo, zero, q)
            s = _dot_nt(qh.astype(BF16), kb) * intra[c]
            inner.append(_dot(s.astype(BF16), vb))
        y = y + jnp.where(lo, inner[0], inner[1])
        ylo = jnp.where(lo, y, 0.0)
        s_lo = ylo.sum(axis=-1, keepdims=True)
        s_hi = (y - ylo).sum(axis=-1, keepdims=True)
        yc = y - jnp.where(lo, s_lo, s_hi) * (1.0 / HEAD_DIM)
        sq = yc * yc
        sqlo = jnp.where(lo, sq, 0.0)
        v_lo = sqlo.sum(axis=-1, keepdims=True)
        v_hi = (sq - sqlo).sum(axis=-1, keepdims=True)
        var = jnp.where(lo, v_lo, v_hi) * (1.0 / HEAD_DIM)
        yn = yc * lax.rsqrt(var + NORM_EPS)
        o_ref[sl, :] = (g * jax.nn.sigmoid(g) * yn).astype(o_ref.dtype)
        return cdec_f * s_f + state_of(k * kdec_f, v)

    lax.fori_loop(0, n_chunks, fwd, s_f0)


def _retention(pb, dec_f, dec_b, batch, seq, ctx_len, latent):
    n_tok = seq if latent else ctx_len
    blk0 = 0 if latent else batch * seq // ctx_len
    ctx_blk0 = batch * seq // ctx_len

    def tok(col):
        return pl.BlockSpec((n_tok, LANES), lambda b, p: (blk0 + b, col + p))

    def ctx(col):
        return pl.BlockSpec((ctx_len, LANES), lambda b, p: (ctx_blk0 + b, col + p))

    dec = pl.BlockSpec((1, 1, LANES), lambda b, p: (p, 0, 0))
    in_specs = [tok(0), tok(2), tok(4), tok(6)]
    args = [pb, pb, pb, pb]
    if latent:
        in_specs += [ctx(2), ctx(4)]
        args += [pb, pb]
    in_specs += [dec, dec]
    args += [dec_f, dec_b]
    return pl.pallas_call(
        functools.partial(_retention_kernel, n_chunks=n_tok // RET_CHUNK, with_ctx_state=latent),
        grid=(batch, 2),
        in_specs=in_specs,
        out_specs=pl.BlockSpec((n_tok, LANES), lambda b, p: (b, p)),
        out_shape=jax.ShapeDtypeStruct((batch * n_tok, 2 * LANES), BF16),
        scratch_shapes=[pltpu.VMEM((n_tok, LANES), F32)],
        compiler_params=_params("arbitrary", "arbitrary"),
        name="retention_latent" if latent else "retention_ctx",
    )(*args)


def _window_attn_kernel(q_ref, k_ref, v_ref, kc_ref, vc_ref, sink_ref, o_ref, *, n_blocks):
    n = pl.program_id(2)
    W = WINDOW
    q = q_ref[...]
    lo = _lane_lo(q.shape)
    zero = jnp.zeros_like(q)
    qs = jnp.concatenate([jnp.where(lo, q, zero), jnp.where(lo, zero, q)], axis=0)
    sk = sink_ref[0]
    sink = jnp.concatenate([jnp.broadcast_to(sk[:, 0:1], (W, 1)),
                            jnp.broadcast_to(sk[:, HEAD_DIM:HEAD_DIM + 1], (W, 1))], axis=0)
    kc, vc = kc_ref[...], vc_ref[...]

    def finish(scores, values):
        m = sink
        for s in scores:
            m = jnp.maximum(m, s.max(axis=-1, keepdims=True))
        den = jnp.exp(sink - m)
        o = None
        for s, v in zip(scores, values):
            p = jnp.exp(s - m)
            den = den + p.sum(axis=-1, keepdims=True)
            t = _dot(p.astype(BF16), v)
            o = t if o is None else o + t
        o = o * (1.0 / den)
        o_ref[...] = jnp.where(lo, o[:W], o[W:]).astype(o_ref.dtype)

    @pl.when(n < n_blocks)
    def _():
        def blk(i):
            return pl.ds(pl.multiple_of(i * W, W), W)

        prv, nxt = blk(jnp.maximum(n - 1, 0)), blk(jnp.minimum(n + 1, n_blocks - 1))
        qi = lax.broadcasted_iota(jnp.int32, (2 * W, W), 0) % W
        kj = lax.broadcasted_iota(jnp.int32, (2 * W, W), 1)
        neg = -jnp.inf
        s_prev = jnp.where((kj >= qi) & (n > 0), _dot_nt(qs, k_ref[prv, :]), neg)
        s_cur = _dot_nt(qs, k_ref[blk(n), :])
        s_next = jnp.where((kj <= qi) & (n < n_blocks - 1), _dot_nt(qs, k_ref[nxt, :]), neg)
        s_ctx = _dot_nt(qs, kc)
        finish([s_prev, s_cur, s_next, s_ctx], [v_ref[prv, :], v_ref[blk(n), :], v_ref[nxt, :], vc])

    @pl.when(n >= n_blocks)
    def _():
        finish([_dot_nt(qs, kc)], [vc])


def _window_attn(pc, sink, batch, seq, ctx_len, need_ctx):
    rows = pc.shape[0]
    n_blocks = seq // WINDOW
    ctx_blocks = ctx_len // WINDOW
    nq = n_blocks + (ctx_blocks if need_ctx else 0)
    ctx_q0 = batch * seq // WINDOW
    ctx_k0 = batch * seq // ctx_len

    def q_idx(b, h, n):
        return (jnp.where(n < n_blocks, b * n_blocks + n, ctx_q0 + b * ctx_blocks + n - n_blocks), h)

    return pl.pallas_call(
        functools.partial(_window_attn_kernel, n_blocks=n_blocks),
        grid=(batch, 2, nq),
        in_specs=[pl.BlockSpec((WINDOW, LANES), q_idx),
                  pl.BlockSpec((seq, LANES), lambda b, h, n: (b, 2 + h)),
                  pl.BlockSpec((seq, LANES), lambda b, h, n: (b, 4 + h)),
                  pl.BlockSpec((ctx_len, LANES), lambda b, h, n: (ctx_k0 + b, 2 + h)),
                  pl.BlockSpec((ctx_len, LANES), lambda b, h, n: (ctx_k0 + b, 4 + h)),
                  pl.BlockSpec((1, 1, LANES), lambda b, h, n: (h, 0, 0))],
        out_specs=pl.BlockSpec((WINDOW, LANES), q_idx),
        out_shape=jax.ShapeDtypeStruct((rows, 2 * LANES), BF16),
        compiler_params=_params("arbitrary", "arbitrary", "arbitrary"),
        name="window_attn",
    )(pc, pc, pc, pc, pc, sink)


def _out_mlp_kernel(*refs, n_lat_tiles, has_ctx, final_norm):
    if has_ctx:
        x_ref, a_ref, bl_ref, bc_ref, c_ref, mod_ref, g_ref, wo_ref, w1_ref, w2_ref = refs[:10]
        rest = refs[10:]
    else:
        x_ref, a_ref, bl_ref, c_ref, mod_ref, g_ref, wo_ref, w1_ref, w2_ref = refs[:9]
        rest = refs[9:]
    if final_norm:
        gf_ref, o_ref = rest
    else:
        (o_ref,) = rest
    mod = mod_ref[0]
    ob = bl_ref[...]
    if has_ctx:
        ob = jnp.where(pl.program_id(0) < n_lat_tiles, ob, bc_ref[...])
    na, nb = a_ref.shape[1], ob.shape[1]
    mix = (_dot(a_ref[...], wo_ref[0:na, :]) + _dot(ob, wo_ref[na:na + nb, :])
           + _dot(c_ref[...], wo_ref[na + nb:, :]))
    x1 = x_ref[...] + mod[2:3] * mix
    h = _modulated_norm(x1, g_ref[...], mod[3:4], mod[4:5]).astype(BF16)
    n_ff = w1_ref.shape[1]
    step = D_MODEL
    acc = None
    for c0 in range(0, n_ff, step):
        a = jnp.maximum(_dot(h, w1_ref[:, c0:c0 + step]), 0.0)
        t = _dot((a * a).astype(BF16), w2_ref[c0:c0 + step, :])
        acc = t if acc is None else acc + t
    x2 = x1 + mod[5:6] * acc
    if final_norm:
        ms = jnp.mean(x2 * x2, axis=-1, keepdims=True)
        x2 = x2 * lax.rsqrt(ms + NORM_EPS) * gf_ref[...]
    o_ref[...] = x2


def _out_mlp(xs, oa, ob_lat, ob_ctx, oc, mods, g, wo, w1, w2, g_final, n_lat_tiles, tiles_per_batch,
             n_ctx_mod):
    d = xs.shape[1]
    has_ctx = ob_ctx is not None
    final_norm = g_final is not None
    n_tiles = (xs.shape[0] // ROW_TILE) if has_ctx else n_lat_tiles

    def mod_idx(i):
        return (jnp.where(i < n_lat_tiles, i // tiles_per_batch, n_ctx_mod), 0, 0)

    def row(width):
        return pl.BlockSpec((ROW_TILE, width), lambda i: (i, 0))

    in_specs = [row(d), row(oa.shape[1]),
                pl.BlockSpec((ROW_TILE, ob_lat.shape[1]), lambda i: (jnp.minimum(i, n_lat_tiles - 1), 0))]
    args = [xs, oa, ob_lat]
    if has_ctx:
        in_specs.append(pl.BlockSpec((ROW_TILE, ob_ctx.shape[1]),
                                     lambda i: (jnp.maximum(i - n_lat_tiles, 0), 0)))
        args.append(ob_ctx)
    in_specs += [row(oc.shape[1]), pl.BlockSpec((1, 6, d), mod_idx), _const_spec((1, d)),
                 _const_spec(wo.shape), _const_spec(w1.shape), _const_spec(w2.shape)]
    args += [oc, mods, g, wo, w1, w2]
    if final_norm:
        in_specs.append(_const_spec((1, d)))
        args.append(g_final)
    return pl.pallas_call(
        functools.partial(_out_mlp_kernel, n_lat_tiles=n_lat_tiles, has_ctx=has_ctx,
                          final_norm=final_norm),
        grid=(n_tiles,),
        in_specs=in_specs,
        out_specs=row(d),
        out_shape=jax.ShapeDtypeStruct((n_tiles * ROW_TILE, d), F32),
        compiler_params=_params("arbitrary"),
        name="out_mlp",
    )(*args)


def _rope_tables(seq):
    rows = seq // GRID_W
    r = jnp.repeat(jnp.arange(rows, dtype=F32), GRID_W)
    col = jnp.tile(jnp.arange(GRID_W, dtype=F32), rows)
    nf = HEAD_DIM // 4
    inv = ROPE_BASE ** (-jnp.arange(nf, dtype=F32) / nf)
    ar, ac = r[:, None] * inv, col[:, None] * inv
    ang = jnp.concatenate([ar, ar, ac, ac], axis=-1)
    ang = jnp.concatenate([ang, ang], axis=-1)
    cos, sin = jnp.cos(ang), jnp.sin(ang)
    up = (jnp.arange(LANES) // nf) % 2 == 0
    sin_up = jnp.where(up, -sin, 0.0)
    sin_dn = jnp.where(up, 0.0, sin)
    ident = jnp.zeros((ROW_TILE, LANES), F32)
    return (jnp.concatenate([cos, ident + 1.0]), jnp.concatenate([sin_up, ident]),
            jnp.concatenate([sin_dn, ident]))


def kernel(x, c, ctx, c_ctx, w_ada, b_ada, g_mix, g_mlp, w_in, w_out, lam_q1, lam_k1, lam_q2, lam_k2,
           subln_g, ret_decay_fwd, ret_decay_bwd, sink_logit, w_mlp1, w_mlp2, g_final):
    batch, seq, d = x.shape
    ctx_len = ctx.shape[1]
    depth = w_ada.shape[0]
    assert d == D_MODEL and seq % ROW_TILE == 0 and (batch * ctx_len) % ROW_TILE == 0
    assert ctx_len == Q_TILE and ctx_len % WINDOW == 0 and seq % GRID_W == 0
    n_lat_tiles = batch * seq // ROW_TILE
    tiles_per_batch = seq // ROW_TILE

    mod_rows = 16
    cc = jnp.concatenate([c, c_ctx[None], jnp.zeros((mod_rows - batch - 1, d), F32)], axis=0)
    mods = _ada_params(cc, w_ada, b_ada).reshape(depth, mod_rows, 6, d)
    cos, sup, sdn = _rope_tables(seq)

    xs = jnp.concatenate([x.reshape(batch * seq, d), ctx.reshape(batch * ctx_len, d)], axis=0)
    for l in range(depth):
        need_ctx = l < depth - 1
        last = l == depth - 1
        lam_init = 0.8 - 0.6 * math.exp(-0.3 * l)
        pa, pb, pc = _in_proj(xs, mods[l], g_mix[l][None], w_in[l].astype(BF16), cos, sup, sdn,
                              n_lat_tiles, tiles_per_batch, batch)
        lamv = jnp.stack([lam_q1[l], lam_k1[l], lam_q2[l], lam_k2[l]])
        oa = _diff_attn(pa, lamv, subln_g[l][None], lam_init, batch, seq, ctx_len, need_ctx)
        dec_f = jnp.repeat(ret_decay_fwd[l], HEAD_DIM).reshape(2, 1, LANES)
        dec_b = jnp.repeat(ret_decay_bwd[l], HEAD_DIM).reshape(2, 1, LANES)
        ob_lat = _retention(pb, dec_f, dec_b, batch, seq, ctx_len, True)
        ob_ctx = _retention(pb, dec_f, dec_b, batch, seq, ctx_len, False) if need_ctx else None
        sink = jnp.repeat(sink_logit[l], HEAD_DIM).reshape(2, 1, LANES)
        oc = _window_attn(pc, sink, batch, seq, ctx_len, need_ctx)
        xs = _out_mlp(xs, oa, ob_lat, ob_ctx, oc, mods[l], g_mlp[l][None], w_out[l].astype(BF16),
                      w_mlp1[l].astype(BF16), w_mlp2[l].astype(BF16),
                      g_final[None] if last else None, n_lat_tiles, tiles_per_batch, batch)
    return xs.reshape(batch, seq, d)
```

```python
import functools
import math

import jax
import jax.numpy as jnp
from jax import lax
from jax.experimental import pallas as pl
from jax.experimental.pallas import tpu as pltpu

D_MODEL = 1024
HEAD_DIM = 64
LANES = 128
GRID_W = 64
N_A = 4
RET_CHUNK = 128
WINDOW = 128
D_FF = 4 * D_MODEL
ROPE_BASE = 10000.0
NORM_EPS = 1e-6
ROW_TILE = 512
Q_SUB = 256
Q_TILE = 2 * Q_SUB
VMEM_LIMIT = 56 * 1024 * 1024

PA_W = 3 * N_A * 2 * HEAD_DIM
PB_W = 4 * 4 * HEAD_DIM
PC_W = 6 * LANES

BF16 = jnp.bfloat16
F32 = jnp.float32


def _dot(a, b):
    return jnp.dot(a, b, preferred_element_type=F32)


def _dot_nt(a, b):
    return lax.dot_general(a, b, (((1,), (1,)), ((), ())), preferred_element_type=F32)


def _dot_tn(a, b):
    return lax.dot_general(a, b, (((0,), (0,)), ((), ())), preferred_element_type=F32)


def _lane_lo(shape):
    return lax.broadcasted_iota(jnp.int32, shape, len(shape) - 1) < HEAD_DIM


def _split_heads(a):
    lo = _lane_lo(a.shape)
    zero = jnp.zeros_like(a)
    return jnp.where(lo, a, zero), jnp.where(lo, zero, a)


def _params(*sem):
    return pltpu.CompilerParams(dimension_semantics=sem, vmem_limit_bytes=VMEM_LIMIT)


def _const_spec(shape):
    nd = len(shape)
    return pl.BlockSpec(shape, lambda *_: (0,) * nd, pipeline_mode=pl.Buffered(1))


def _ada_kernel(cc_ref, w_ref, b_ref, o_ref):
    cc = cc_ref[...]
    a = (cc * jax.nn.sigmoid(cc)).astype(BF16)
    o_ref[0] = _dot(a, w_ref[0].astype(BF16)) + b_ref[0]


def _ada_params(cc, w_ada, b_ada):
    depth, d, n6 = w_ada.shape
    tn = n6 // 4
    rows = cc.shape[0]
    return pl.pallas_call(
        _ada_kernel,
        grid=(depth, n6 // tn),
        in_specs=[pl.BlockSpec((rows, d), lambda l, j: (0, 0)),
                  pl.BlockSpec((1, d, tn), lambda l, j: (l, 0, j)),
                  pl.BlockSpec((1, 1, tn), lambda l, j: (l, 0, j))],
        out_specs=pl.BlockSpec((1, rows, tn), lambda l, j: (l, 0, j)),
        out_shape=jax.ShapeDtypeStruct((depth, rows, n6), F32),
        compiler_params=_params("arbitrary", "arbitrary"),
        name="ada_params",
    )(cc, w_ada, b_ada.reshape(depth, 1, n6))


def _modulated_norm(x, g, shift, scale):
    ms = jnp.mean(x * x, axis=-1, keepdims=True)
    return (x * lax.rsqrt(ms + NORM_EPS) * g) * (1.0 + scale) + shift


def _rope(a, cos, sin_up, sin_dn):
    return (a * cos + pltpu.roll(a, LANES - 16, axis=1) * sin_up
            + pltpu.roll(a, 16, axis=1) * sin_dn)


_A_W = N_A * 2 * HEAD_DIM
_SEGS_A = ((0, 4, True, HEAD_DIM ** -0.5), (_A_W, 4, True, 1.0), (2 * _A_W, 4, False, 1.0))
_B0 = 3 * _A_W
_SEGS_B = ((_B0, 2, True, 1.0), (_B0 + 256, 2, True, HEAD_DIM ** -0.5),
           (_B0 + 512, 2, False, 1.0), (_B0 + 768, 2, False, 1.0))
_C0 = _B0 + 1024


def _in_proj_kernel(x_ref, mod_ref, g_ref, w_ref, cos_ref, sup_ref, sdn_ref, pa_ref, pb_ref, pc_ref):
    mod = mod_ref[0]
    h = _modulated_norm(x_ref[...], g_ref[...], mod[0:1], mod[1:2]).astype(BF16)
    cos, sup, sdn = cos_ref[...], sup_ref[...], sdn_ref[...]

    def project(col0, nblk):
        return _dot(h, w_ref[:, col0:col0 + nblk * LANES])

    def finish(acc, j, rope, mult):
        a = acc[:, j * LANES:(j + 1) * LANES]
        if rope:
            a = _rope(a, cos, sup, sdn)
        if mult != 1.0:
            a = a * mult
        return a

    for segs, ref in ((_SEGS_A, pa_ref), (_SEGS_B, pb_ref)):
        out_col = 0
        for col0, nblk, rope, mult in segs:
            acc = project(col0, nblk)
            for j in range(nblk):
                ref[:, out_col:out_col + LANES] = finish(acc, j, rope, mult).astype(ref.dtype)
                out_col += LANES

    acc = project(_C0, 4)
    lo = _lane_lo((acc.shape[0], LANES))
    for j in range(2):
        pc_ref[:, j * LANES:(j + 1) * LANES] = finish(acc, j, True, HEAD_DIM ** -0.5).astype(BF16)
    for j, rope in ((2, True), (3, False)):
        a = finish(acc, j, rope, 1.0)
        r = pltpu.roll(a, HEAD_DIM, axis=1)
        base = (2 + 2 * (j - 2)) * LANES
        pc_ref[:, base:base + LANES] = jnp.where(lo, a, r).astype(BF16)
        pc_ref[:, base + LANES:base + 2 * LANES] = jnp.where(lo, r, a).astype(BF16)


def _in_proj(xs, mods, g, w, cos, sup, sdn, n_lat_tiles, tiles_per_batch, n_ctx_mod):
    rows, d = xs.shape
    n_tiles = rows // ROW_TILE
    dproj = w.shape[1]

    def mod_idx(i):
        return (jnp.where(i < n_lat_tiles, i // tiles_per_batch, n_ctx_mod), 0, 0)

    def tab_idx(i):
        return (jnp.where(i < n_lat_tiles, i % tiles_per_batch, tiles_per_batch), 0)

    tab = pl.BlockSpec((ROW_TILE, LANES), tab_idx)
    return pl.pallas_call(
        _in_proj_kernel,
        grid=(n_tiles,),
        in_specs=[pl.BlockSpec((ROW_TILE, d), lambda i: (i, 0)),
                  pl.BlockSpec((1, 6, d), mod_idx),
                  _const_spec((1, d)),
                  _const_spec((d, dproj)),
                  tab, tab, tab],
        out_specs=[pl.BlockSpec((ROW_TILE, PA_W), lambda i: (i, 0)),
                   pl.BlockSpec((ROW_TILE, PB_W), lambda i: (i, 0)),
                   pl.BlockSpec((ROW_TILE, PC_W), lambda i: (i, 0))],
        out_shape=[jax.ShapeDtypeStruct((rows, PA_W), BF16),
                   jax.ShapeDtypeStruct((rows, PB_W), F32),
                   jax.ShapeDtypeStruct((rows, PC_W), BF16)],
        compiler_params=_params("arbitrary"),
        name="in_proj",
    )(xs, mods, g, w, cos, sup, sdn)


def _diff_attn_kernel(*refs, lam_init, need_ctx):
    if need_ctx:
        (q_ref, kx_ref, vx_ref, kc_ref, vc_ref, qc_ref, lam_ref, g_ref, o_ref, oc_ref,
         vxt_ref, vct_ref, s_ref, acc_ref) = refs
    else:
        (q_ref, kx_ref, vx_ref, kc_ref, vc_ref, lam_ref, g_ref, o_ref,
         vxt_ref, vct_ref, s_ref, acc_ref) = refs
    seq, n_ctx = kx_ref.shape[0], kc_ref.shape[0]
    KC = n_ctx
    lv = lam_ref[...]
    lam = (jnp.exp(jnp.sum(lv[0:1] * lv[1:2], axis=-1, keepdims=True))
           - jnp.exp(jnp.sum(lv[2:3] * lv[3:4], axis=-1, keepdims=True)) + lam_init)
    gain = g_ref[...] * (1.0 - lam_init)

    def fold8(a, op):
        return op(a.reshape(a.shape[0] // 8, 8, a.shape[1]), axis=0)

    def scores(u, qs, k_blk, row0, m):
        out = []
        for c in range(2):
            s = _dot_nt(k_blk, qs[c])
            s_ref[2 * u + c, row0:row0 + KC, :] = s
            part = fold8(s, jnp.max)
            out.append(part if m is None else jnp.maximum(m[c], part))
        return tuple(out)

    def weigh(u, vt_blk, row0, m, den):
        out = []
        for c in range(2):
            p = jnp.exp(s_ref[2 * u + c, row0:row0 + KC, :] - m[c])
            part = fold8(p, jnp.sum)
            out.append(part if den is None else den[c] + part)
            t = _dot(vt_blk, p.astype(BF16))
            if den is None:
                acc_ref[2 * u + c] = t
            else:
                acc_ref[2 * u + c] += t
        return tuple(out)

    def finish(u, den, dst_ref, r0):
        inv0 = 1.0 / den[0].sum(axis=0, keepdims=True)
        inv1 = lam / den[1].sum(axis=0, keepdims=True)
        o = acc_ref[2 * u] * inv0 - acc_ref[2 * u + 1] * inv1
        ms = jnp.mean(o * o, axis=0, keepdims=True)
        o = o * lax.rsqrt(ms + NORM_EPS) * gain
        dst_ref[r0:r0 + Q_SUB, :] = o.T.astype(dst_ref.dtype)

    def col_max(m):
        return tuple(t.max(axis=0, keepdims=True) for t in m)

    chunks = [(lambda: kc_ref[...], lambda: vct_ref[...], seq)]
    for i in range(seq // KC):
        chunks.append((lambda i=i: kx_ref[i * KC:(i + 1) * KC, :],
                       lambda i=i: vxt_ref[:, i * KC:(i + 1) * KC], i * KC))

    @pl.when(pl.program_id(2) == 0)
    def _():
        vxt_ref[...] = vx_ref[...].T
        vct_ref[...] = vc_ref[...].T
        if need_ctx:
            qs = _split_heads(qc_ref[...])
            k_of, vt_of, row0 = chunks[0]
            m = col_max(scores(0, qs, k_of(), row0, None))
            finish(0, weigh(0, vt_of(), row0, m, None), oc_ref, 0)

    qa = _split_heads(q_ref[0:Q_SUB, :])
    qb = _split_heads(q_ref[Q_SUB:2 * Q_SUB, :])
    ma = None
    for k_of, _, row0 in chunks:
        ma = scores(0, qa, k_of(), row0, ma)
    ma = col_max(ma)
    mb, da = None, None
    for k_of, vt_of, row0 in chunks:
        mb = scores(1, qb, k_of(), row0, mb)
        da = weigh(0, vt_of(), row0, ma, da)
    finish(0, da, o_ref, 0)
    mb = col_max(mb)
    db = None
    for _, vt_of, row0 in chunks:
        db = weigh(1, vt_of(), row0, mb, db)
    finish(1, db, o_ref, Q_SUB)


def _diff_attn(pa, lamv, g, lam_init, batch, seq, ctx_len, need_ctx):
    n_q = seq // Q_TILE
    ctx_k0 = batch * seq // ctx_len

    def lat(col):
        return pl.BlockSpec((seq, LANES), lambda b, h, t: (b, col + h))

    def ctx(col):
        return pl.BlockSpec((ctx_len, LANES), lambda b, h, t: (ctx_k0 + b, col + h))

    in_specs = [pl.BlockSpec((Q_TILE, LANES), lambda b, h, t: (b * n_q + t, h)),
                lat(N_A), lat(2 * N_A), ctx(N_A), ctx(2 * N_A)]
    args = [pa, pa, pa, pa, pa]
    out_specs = [pl.BlockSpec((Q_TILE, LANES), lambda b, h, t: (b * n_q + t, h))]
    out_shape = [jax.ShapeDtypeStruct((batch * seq, N_A * LANES), BF16)]
    if need_ctx:
        in_specs.append(ctx(0))
        args.append(pa)
        out_specs.append(pl.BlockSpec((ctx_len, LANES), lambda b, h, t: (b, h)))
        out_shape.append(jax.ShapeDtypeStruct((batch * ctx_len, N_A * LANES), BF16))
    in_specs += [_const_spec((4, HEAD_DIM)), _const_spec((LANES, 1))]
    args += [lamv, g]
    out = pl.pallas_call(
        functools.partial(_diff_attn_kernel, lam_init=lam_init, need_ctx=need_ctx),
        grid=(batch, N_A, n_q),
        in_specs=in_specs,
        out_specs=out_specs,
        out_shape=out_shape,
        scratch_shapes=[pltpu.VMEM((LANES, seq), BF16), pltpu.VMEM((LANES, ctx_len), BF16),
                        pltpu.VMEM((4, seq + ctx_len, Q_SUB), F32),
                        pltpu.VMEM((4, LANES, Q_SUB), F32)],
        compiler_params=_params("arbitrary", "arbitrary", "arbitrary"),
        name="diff_attn",
    )(*args)
    return out[0], (out[1] if need_ctx else None)


def _retention_kernel(*refs, need_ctx):
    if need_ctx:
        (q_ref, k_ref, v_ref, g_ref, qc_ref, kc_ref, vc_ref, gc_ref, df_ref, db_ref, o_ref, oc_ref,
         u_ref, st_ref) = refs
    else:
        (q_ref, k_ref, v_ref, g_ref, kc_ref, vc_ref, df_ref, db_ref, o_ref, u_ref, st_ref) = refs
    C = RET_CHUNK
    lgf = jax.nn.log_sigmoid(df_ref[0])
    lgb = jax.nn.log_sigmoid(db_ref[0])
    row = lax.broadcasted_iota(jnp.int32, (C, LANES), 0).astype(F32)
    qdec = jnp.concatenate([jnp.exp((row + 1.0) * lgf), jnp.exp((C - row) * lgb)], axis=1)
    kdec_f = jnp.exp((C - 1.0 - row) * lgf)
    kdec_b = jnp.exp(row * lgb)
    cdec_f = jnp.exp(C * lgf)
    cdec_b = jnp.exp(C * lgb)
    ii = lax.broadcasted_iota(jnp.int32, (C, C), 0)
    mm = lax.broadcasted_iota(jnp.int32, (C, C), 1)
    dist = (ii - mm).astype(F32)
    causal = ii >= mm
    fwd_d = jnp.where(causal, dist, 0.0)
    bwd_d = jnp.where(causal, 0.0, -dist)
    intra = jnp.concatenate(
        [jnp.where(causal, jnp.exp(fwd_d * lgf[:, c:c + 1]), jnp.exp(bwd_d * lgb[:, c:c + 1]))
         for c in (0, HEAD_DIM)], axis=0)
    lo = _lane_lo((C, LANES))
    same_head = lo == (lax.broadcasted_iota(jnp.int32, (C, LANES), 0) < HEAD_DIM)

    def state_of(k, v):
        return jnp.where(same_head, _dot_tn(k.astype(BF16), v.astype(BF16)), 0.0)

    def sweep(q_ref, k_ref, v_ref, g_ref, o_ref, f0, g0):
        n = q_ref.shape[0] // C
        for j in range(n):
            k, v = k_ref[j * C:(j + 1) * C, :], v_ref[j * C:(j + 1) * C, :]
            u_ref[0, j] = state_of(k * kdec_f, v)
            u_ref[1, j] = state_of(k * kdec_b, v)
        s = f0
        for j in range(n):
            st_ref[j, 0:LANES, :] = s.astype(BF16)
            if j + 1 < n:
                s = cdec_f * s + u_ref[0, j]
        s = g0
        for j in reversed(range(n)):
            st_ref[j, LANES:2 * LANES, :] = s.astype(BF16)
            if j > 0:
                s = cdec_b * s + u_ref[1, j]
        for j in range(n):
            sl = slice(j * C, (j + 1) * C)
            q, k, v, g = q_ref[sl, :], k_ref[sl, :], v_ref[sl, :], g_ref[sl, :]
            q2 = jnp.concatenate([q, q], axis=1) * qdec
            y = _dot(q2.astype(BF16), st_ref[j])
            qh = jnp.concatenate(_split_heads(q.astype(BF16)), axis=0)
            s = _dot_nt(qh, k.astype(BF16)) * intra
            inner = _dot(s.astype(BF16), v.astype(BF16))
            y = y + jnp.where(lo, inner[:C], inner[C:])
            ylo = jnp.where(lo, y, 0.0)
            s_lo = ylo.sum(axis=-1, keepdims=True)
            s_hi = (y - ylo).sum(axis=-1, keepdims=True)
            yc = y - jnp.where(lo, s_lo, s_hi) * (1.0 / HEAD_DIM)
            sq = yc * yc
            sqlo = jnp.where(lo, sq, 0.0)
            v_lo = sqlo.sum(axis=-1, keepdims=True)
            v_hi = (sq - sqlo).sum(axis=-1, keepdims=True)
            var = jnp.where(lo, v_lo, v_hi) * (1.0 / HEAD_DIM)
            yn = yc * lax.rsqrt(var + NORM_EPS)
            o_ref[sl, :] = (g * jax.nn.sigmoid(g) * yn).astype(o_ref.dtype)

    n_ctx = kc_ref.shape[0]
    pos = lax.broadcasted_iota(jnp.int32, (n_ctx, LANES), 0).astype(F32)
    kc, vc = kc_ref[...], vc_ref[...]
    f0 = state_of(kc * jnp.exp((n_ctx - 1.0 - pos) * lgf), vc)
    g0 = state_of(kc * jnp.exp(pos * lgb), vc)
    sweep(q_ref, k_ref, v_ref, g_ref, o_ref, f0, g0)
    if need_ctx:
        zero = jnp.zeros((LANES, LANES), F32)
        sweep(qc_ref, kc_ref, vc_ref, gc_ref, oc_ref, zero, zero)


def _retention(pb, dec_f, dec_b, batch, seq, ctx_len, need_ctx):
    ctx_blk0 = batch * seq // ctx_len

    def lat(col):
        return pl.BlockSpec((seq, LANES), lambda b, p: (b, col + p))

    def ctx(col):
        return pl.BlockSpec((ctx_len, LANES), lambda b, p: (ctx_blk0 + b, col + p))

    dec = pl.BlockSpec((1, 1, LANES), lambda b, p: (p, 0, 0))
    in_specs = [lat(0), lat(2), lat(4), lat(6)]
    in_specs += [ctx(0), ctx(2), ctx(4), ctx(6)] if need_ctx else [ctx(2), ctx(4)]
    in_specs += [dec, dec]
    args = [pb] * (len(in_specs) - 2) + [dec_f, dec_b]
    out_specs = [pl.BlockSpec((seq, LANES), lambda b, p: (b, p))]
    out_shape = [jax.ShapeDtypeStruct((batch * seq, 2 * LANES), BF16)]
    if need_ctx:
        out_specs.append(pl.BlockSpec((ctx_len, LANES), lambda b, p: (b, p)))
        out_shape.append(jax.ShapeDtypeStruct((batch * ctx_len, 2 * LANES), BF16))
    n_chunks = seq // RET_CHUNK
    out = pl.pallas_call(
        functools.partial(_retention_kernel, need_ctx=need_ctx),
        grid=(batch, 2),
        in_specs=in_specs,
        out_specs=out_specs,
        out_shape=out_shape,
        scratch_shapes=[pltpu.VMEM((2, n_chunks, LANES, LANES), F32),
                        pltpu.VMEM((n_chunks, 2 * LANES, LANES), BF16)],
        compiler_params=_params("arbitrary", "arbitrary"),
        name="retention",
    )(*args)
    return out[0], (out[1] if need_ctx else None)


def _window_attn_kernel(*refs, need_ctx):
    if need_ctx:
        q_ref, k_ref, v_ref, kc_ref, vc_ref, qc_ref, sink_ref, o_ref, oc_ref = refs
    else:
        q_ref, k_ref, v_ref, kc_ref, vc_ref, sink_ref, o_ref = refs
    W = WINDOW
    n_blocks = q_ref.shape[0] // W
    lo = _lane_lo((W, LANES))
    sk = sink_ref[0]
    sink = jnp.concatenate([jnp.broadcast_to(sk[:, 0:1], (W, 1)),
                            jnp.broadcast_to(sk[:, HEAD_DIM:HEAD_DIM + 1], (W, 1))], axis=0)
    kc, vc = kc_ref[...], vc_ref[...]
    qi = lax.broadcasted_iota(jnp.int32, (2 * W, W), 0) % W
    kj = lax.broadcasted_iota(jnp.int32, (2 * W, W), 1)
    keep = {"prev": kj >= qi, "next": kj <= qi}

    def lane_blocks(a):
        return [a[:, c:c + LANES] for c in range(0, a.shape[1], LANES)]

    def attend(q, win, kinds, dst_ref, r0):
        qs = jnp.concatenate(_split_heads(q), axis=0)
        s_ctx = _dot_nt(qs, kc)
        blocks = lane_blocks(s_ctx)
        if win is not None:
            s_win = _dot_nt(qs, k_ref[win, :])
            for kind, blk in zip(kinds, lane_blocks(s_win)):
                blocks.append(blk if kind == "cur" else jnp.where(keep[kind], blk, -jnp.inf))
        top = blocks[0]
        for blk in blocks[1:]:
            top = jnp.maximum(top, blk)
        m = jnp.maximum(top.max(axis=-1, keepdims=True), sink)
        p = [jnp.exp(blk - m) for blk in blocks]
        tot = p[0]
        for blk in p[1:]:
            tot = tot + blk
        den = tot.sum(axis=-1, keepdims=True) + jnp.exp(sink - m)
        n_c = s_ctx.shape[1] // LANES
        o = _dot(jnp.concatenate(p[:n_c], axis=1).astype(BF16), vc)
        if win is not None:
            o = o + _dot(jnp.concatenate(p[n_c:], axis=1).astype(BF16), v_ref[win, :])
        o = o * (1.0 / den)
        dst_ref[r0:r0 + W, :] = jnp.where(lo, o[:W], o[W:]).astype(dst_ref.dtype)

    for n in range(n_blocks):
        first, last = max(n - 1, 0), min(n + 1, n_blocks - 1)
        kinds = (["prev"] if n > 0 else []) + ["cur"] + (["next"] if n < n_blocks - 1 else [])
        attend(q_ref[n * W:(n + 1) * W, :], slice(first * W, (last + 1) * W), kinds, o_ref, n * W)
    if need_ctx:
        for n in range(qc_ref.shape[0] // W):
            attend(qc_ref[n * W:(n + 1) * W, :], None, None, oc_ref, n * W)


def _window_attn(pc, sink, batch, seq, ctx_len, need_ctx):
    ctx_k0 = batch * seq // ctx_len

    def lat(col):
        return pl.BlockSpec((seq, LANES), lambda b, h: (b, col + h))

    def ctx(col):
        return pl.BlockSpec((ctx_len, LANES), lambda b, h: (ctx_k0 + b, col + h))

    in_specs = [lat(0), lat(2), lat(4), ctx(2), ctx(4)]
    out_specs = [pl.BlockSpec((seq, LANES), lambda b, h: (b, h))]
    out_shape = [jax.ShapeDtypeStruct((batch * seq, 2 * LANES), BF16)]
    if need_ctx:
        in_specs.append(ctx(0))
        out_specs.append(pl.BlockSpec((ctx_len, LANES), lambda b, h: (b, h)))
        out_shape.append(jax.ShapeDtypeStruct((batch * ctx_len, 2 * LANES), BF16))
    args = [pc] * len(in_specs) + [sink]
    in_specs.append(pl.BlockSpec((1, 1, LANES), lambda b, h: (h, 0, 0)))
    out = pl.pallas_call(
        functools.partial(_window_attn_kernel, need_ctx=need_ctx),
        grid=(batch, 2),
        in_specs=in_specs,
        out_specs=out_specs,
        out_shape=out_shape,
        compiler_params=_params("arbitrary", "arbitrary"),
        name="window_attn",
    )(*args)
    return out[0], (out[1] if need_ctx else None)


def _out_mlp_kernel(*refs, n_lat_tiles, has_ctx, final_norm):
    n_mix = 6 if has_ctx else 3
    x_ref = refs[0]
    mix_refs = refs[1:1 + n_mix]
    mod_ref, g_ref, wo_ref, w1_ref, w2_ref = refs[1 + n_mix:6 + n_mix]
    rest = refs[6 + n_mix:]
    if final_norm:
        gf_ref, o_ref = rest
    else:
        (o_ref,) = rest
    mod = mod_ref[0]
    is_lat = pl.program_id(0) < n_lat_tiles
    mix, r0 = None, 0
    for i in range(3):
        a = mix_refs[i][...]
        if has_ctx:
            a = jnp.where(is_lat, a, mix_refs[3 + i][...])
        t = _dot(a, wo_ref[r0:r0 + a.shape[1], :])
        mix = t if mix is None else mix + t
        r0 += a.shape[1]
    x1 = x_ref[...] + mod[2:3] * mix
    h = _modulated_norm(x1, g_ref[...], mod[3:4], mod[4:5]).astype(BF16)
    n_ff = w1_ref.shape[1]
    step = D_MODEL
    acc = None
    for c0 in range(0, n_ff, step):
        a = jnp.maximum(_dot(h, w1_ref[:, c0:c0 + step]), 0.0)
        t = _dot((a * a).astype(BF16), w2_ref[c0:c0 + step, :])
        acc = t if acc is None else acc + t
    x2 = x1 + mod[5:6] * acc
    if final_norm:
        ms = jnp.mean(x2 * x2, axis=-1, keepdims=True)
        x2 = x2 * lax.rsqrt(ms + NORM_EPS) * gf_ref[...]
    o_ref[...] = x2


def _out_mlp(xs, lat_mix, ctx_mix, mods, g, wo, w1, w2, g_final, n_lat_tiles, tiles_per_batch,
             n_ctx_mod):
    d = xs.shape[1]
    has_ctx = ctx_mix is not None
    final_norm = g_final is not None
    n_tiles = (xs.shape[0] // ROW_TILE) if has_ctx else n_lat_tiles

    def mod_idx(i):
        return (jnp.where(i < n_lat_tiles, i // tiles_per_batch, n_ctx_mod), 0, 0)

    def row(width):
        return pl.BlockSpec((ROW_TILE, width), lambda i: (i, 0))

    in_specs = [row(d)]
    args = [xs]
    for a in lat_mix:
        in_specs.append(pl.BlockSpec((ROW_TILE, a.shape[1]), lambda i: (jnp.minimum(i, n_lat_tiles - 1), 0)))
        args.append(a)
    if has_ctx:
        for a in ctx_mix:
            in_specs.append(pl.BlockSpec((ROW_TILE, a.shape[1]),
                                         lambda i: (jnp.maximum(i - n_lat_tiles, 0), 0)))
            args.append(a)
    in_specs += [pl.BlockSpec((1, 6, d), mod_idx), _const_spec((1, d)),
                 _const_spec(wo.shape), _const_spec(w1.shape), _const_spec(w2.shape)]
    args += [mods, g, wo, w1, w2]
    if final_norm:
        in_specs.append(_const_spec((1, d)))
        args.append(g_final)
    return pl.pallas_call(
        functools.partial(_out_mlp_kernel, n_lat_tiles=n_lat_tiles, has_ctx=has_ctx,
                          final_norm=final_norm),
        grid=(n_tiles,),
        in_specs=in_specs,
        out_specs=row(d),
        out_shape=jax.ShapeDtypeStruct((n_tiles * ROW_TILE, d), F32),
        compiler_params=_params("arbitrary"),
        name="out_mlp",
    )(*args)


def _rope_tables(seq):
    rows = seq // GRID_W
    r = jnp.repeat(jnp.arange(rows, dtype=F32), GRID_W)
    col = jnp.tile(jnp.arange(GRID_W, dtype=F32), rows)
    nf = HEAD_DIM // 4
    inv = ROPE_BASE ** (-jnp.arange(nf, dtype=F32) / nf)
    ar, ac = r[:, None] * inv, col[:, None] * inv
    ang = jnp.concatenate([ar, ar, ac, ac], axis=-1)
    ang = jnp.concatenate([ang, ang], axis=-1)
    cos, sin = jnp.cos(ang), jnp.sin(ang)
    up = (jnp.arange(LANES) // nf) % 2 == 0
    sin_up = jnp.where(up, -sin, 0.0)
    sin_dn = jnp.where(up, 0.0, sin)
    ident = jnp.zeros((ROW_TILE, LANES), F32)
    return (jnp.concatenate([cos, ident + 1.0]), jnp.concatenate([sin_up, ident]),
            jnp.concatenate([sin_dn, ident]))


def kernel(x, c, ctx, c_ctx, w_ada, b_ada, g_mix, g_mlp, w_in, w_out, lam_q1, lam_k1, lam_q2, lam_k2,
           subln_g, ret_decay_fwd, ret_decay_bwd, sink_logit, w_mlp1, w_mlp2, g_final):
    batch, seq, d = x.shape
    ctx_len = ctx.shape[1]
    depth = w_ada.shape[0]
    assert d == D_MODEL and seq % ROW_TILE == 0 and (batch * ctx_len) % ROW_TILE == 0
    assert ctx_len == Q_SUB and ctx_len % WINDOW == 0 and seq % GRID_W == 0
    n_lat_tiles = batch * seq // ROW_TILE
    tiles_per_batch = seq // ROW_TILE

    mod_rows = 16
    cc = jnp.concatenate([c, c_ctx[None], jnp.zeros((mod_rows - batch - 1, d), F32)], axis=0)
    mods = _ada_params(cc, w_ada, b_ada).reshape(depth, mod_rows, 6, d)
    cos, sup, sdn = _rope_tables(seq)

    xs = jnp.concatenate([x.reshape(batch * seq, d), ctx.reshape(batch * ctx_len, d)], axis=0)
    for l in range(depth):
        need_ctx = l < depth - 1
        last = l == depth - 1
        lam_init = 0.8 - 0.6 * math.exp(-0.3 * l)
        pa, pb, pc = _in_proj(xs, mods[l], g_mix[l][None], w_in[l].astype(BF16), cos, sup, sdn,
                              n_lat_tiles, tiles_per_batch, batch)
        lamv = jnp.stack([lam_q1[l], lam_k1[l], lam_q2[l], lam_k2[l]])
        oa, oa_c = _diff_attn(pa, lamv, subln_g[l][:, None], lam_init, batch, seq, ctx_len, need_ctx)
        dec_f = jnp.repeat(ret_decay_fwd[l], HEAD_DIM).reshape(2, 1, LANES)
        dec_b = jnp.repeat(ret_decay_bwd[l], HEAD_DIM).reshape(2, 1, LANES)
        ob, ob_c = _retention(pb, dec_f, dec_b, batch, seq, ctx_len, need_ctx)
        sink = jnp.repeat(sink_logit[l], HEAD_DIM).reshape(2, 1, LANES)
        oc, oc_c = _window_attn(pc, sink, batch, seq, ctx_len, need_ctx)
        xs = _out_mlp(xs, (oa, ob, oc), (oa_c, ob_c, oc_c) if need_ctx else None, mods[l],
                      g_mlp[l][None], w_out[l].astype(BF16), w_mlp1[l].astype(BF16),
                      w_mlp2[l].astype(BF16), g_final[None] if last else None,
                      n_lat_tiles, tiles_per_batch, batch)
    return xs.reshape(batch, seq, d)
```

```python
import functools
import math

import jax
import jax.numpy as jnp
from jax import lax
from jax.experimental import pallas as pl
from jax.experimental.pallas import tpu as pltpu

D_MODEL = 1024
HEAD_DIM = 64
LANES = 128
GRID_W = 64
N_A = 4
RET_CHUNK = 128
WINDOW = 128
D_FF = 4 * D_MODEL
ROPE_BASE = 10000.0
NORM_EPS = 1e-6
ROW_TILE = 512
Q_SUB = 256
VMEM_LIMIT = 56 * 1024 * 1024
LOG2E = math.log2(math.e)

PA_W = 3 * N_A * 2 * HEAD_DIM
PB_W = 4 * 4 * HEAD_DIM
PC_W = 6 * LANES

BF16 = jnp.bfloat16
F32 = jnp.float32


def _dot(a, b):
    return jnp.dot(a, b, preferred_element_type=F32)


def _dot_nt(a, b):
    return lax.dot_general(a, b, (((1,), (1,)), ((), ())), preferred_element_type=F32)


def _dot_tn(a, b):
    return lax.dot_general(a, b, (((0,), (0,)), ((), ())), preferred_element_type=F32)


def _lane_lo(shape):
    return lax.broadcasted_iota(jnp.int32, shape, len(shape) - 1) < HEAD_DIM


def _split_heads(a):
    lo = _lane_lo(a.shape)
    zero = jnp.zeros_like(a)
    return jnp.where(lo, a, zero), jnp.where(lo, zero, a)


def _params(*sem):
    return pltpu.CompilerParams(dimension_semantics=sem, vmem_limit_bytes=VMEM_LIMIT)


def _const_spec(shape):
    nd = len(shape)
    return pl.BlockSpec(shape, lambda *_: (0,) * nd, pipeline_mode=pl.Buffered(1))


def _layer_spec(arr, layer):
    tail = (0,) * (arr.ndim - 1)
    return pl.BlockSpec((None,) + arr.shape[1:], lambda *_: (layer,) + tail,
                        pipeline_mode=pl.Buffered(1))


def _ada_kernel(cc_ref, w_ref, b_ref, o_ref):
    cc = cc_ref[...]
    a = (cc * jax.nn.sigmoid(cc)).astype(BF16)
    o_ref[0] = _dot(a, w_ref[0].astype(BF16)) + b_ref[0]


def _ada_params(cc, w_ada, b_ada):
    depth, d, n6 = w_ada.shape
    tn = n6 // 4
    rows = cc.shape[0]
    return pl.pallas_call(
        _ada_kernel,
        grid=(depth, n6 // tn),
        in_specs=[pl.BlockSpec((rows, d), lambda l, j: (0, 0)),
                  pl.BlockSpec((1, d, tn), lambda l, j: (l, 0, j)),
                  pl.BlockSpec((1, 1, tn), lambda l, j: (l, 0, j))],
        out_specs=pl.BlockSpec((1, rows, tn), lambda l, j: (l, 0, j)),
        out_shape=jax.ShapeDtypeStruct((depth, rows, n6), F32),
        compiler_params=_params("arbitrary", "arbitrary"),
        name="ada_params",
    )(cc, w_ada, b_ada.reshape(depth, 1, n6))


def _modulated_norm(x, g, shift, scale):
    ms = jnp.mean(x * x, axis=-1, keepdims=True)
    return (x * lax.rsqrt(ms + NORM_EPS) * g) * (1.0 + scale) + shift


def _rope(a, cos, sin_up, sin_dn):
    return (a * cos + pltpu.roll(a, LANES - 16, axis=1) * sin_up
            + pltpu.roll(a, 16, axis=1) * sin_dn)


class _Rows:
    def __init__(self, arrays, n_lat_tiles):
        self.arrays = arrays
        self.n_lat_tiles = n_lat_tiles
        self.n_tiles = sum(a.shape[0] for a in arrays) // ROW_TILE

    def specs(self):
        d = self.arrays[0].shape[1]
        if len(self.arrays) == 1:
            return [pl.BlockSpec((ROW_TILE, d), lambda i: (i, 0))]
        n = self.n_lat_tiles
        return [pl.BlockSpec((ROW_TILE, d), lambda i: (jnp.minimum(i, n - 1), 0)),
                pl.BlockSpec((ROW_TILE, d), lambda i: (jnp.maximum(i - n, 0), 0))]

    def load(self, refs):
        if len(refs) == 1:
            return refs[0][...]
        return jnp.where(pl.program_id(0) < self.n_lat_tiles, refs[0][...], refs[1][...])


_A_W = N_A * 2 * HEAD_DIM
_SEGS_A = ((0, 4, True, HEAD_DIM ** -0.5 * LOG2E), (_A_W, 4, True, 1.0), (2 * _A_W, 4, False, 1.0))
_B0 = 3 * _A_W
_SEGS_B = ((_B0, 2, True, 1.0), (_B0 + 256, 2, True, HEAD_DIM ** -0.5),
           (_B0 + 512, 2, False, 1.0), (_B0 + 768, 2, False, 1.0))
_C0 = _B0 + 1024


def _in_proj_kernel(*refs, rows):
    n_x = len(rows.arrays)
    x_refs = refs[:n_x]
    mod_ref, g_ref, w_ref, cos_ref, sup_ref, sdn_ref, pa_ref, pb_ref, pc_ref = refs[n_x:]
    mod = mod_ref[0]
    h = _modulated_norm(rows.load(x_refs), g_ref[...], mod[0:1], mod[1:2]).astype(BF16)
    cos, sup, sdn = cos_ref[...], sup_ref[...], sdn_ref[...]

    def project(col0, nblk):
        return _dot(h, w_ref[:, col0:col0 + nblk * LANES])

    def finish(acc, j, rope, mult):
        a = acc[:, j * LANES:(j + 1) * LANES]
        if rope:
            a = _rope(a, cos, sup, sdn)
        if mult != 1.0:
            a = a * mult
        return a

    for segs, ref in ((_SEGS_A, pa_ref), (_SEGS_B, pb_ref)):
        out_col = 0
        for col0, nblk, rope, mult in segs:
            acc = project(col0, nblk)
            for j in range(nblk):
                ref[:, out_col:out_col + LANES] = finish(acc, j, rope, mult).astype(ref.dtype)
                out_col += LANES

    acc = project(_C0, 4)
    lo = _lane_lo((acc.shape[0], LANES))
    for j in range(2):
        pc_ref[:, j * LANES:(j + 1) * LANES] = finish(acc, j, True, HEAD_DIM ** -0.5).astype(BF16)
    for j, rope in ((2, True), (3, False)):
        a = finish(acc, j, rope, 1.0)
        r = pltpu.roll(a, HEAD_DIM, axis=1)
        base = (2 + 2 * (j - 2)) * LANES
        pc_ref[:, base:base + LANES] = jnp.where(lo, a, r).astype(BF16)
        pc_ref[:, base + LANES:base + 2 * LANES] = jnp.where(lo, r, a).astype(BF16)


def _in_proj(rows, mods, g, w, layer, cos, sup, sdn, tiles_per_batch, n_ctx_mod):
    d = rows.arrays[0].shape[1]
    n_lat_tiles = rows.n_lat_tiles
    n_rows = rows.n_tiles * ROW_TILE

    def mod_idx(i):
        return (jnp.where(i < n_lat_tiles, i // tiles_per_batch, n_ctx_mod), 0, 0)

    def tab_idx(i):
        return (jnp.where(i < n_lat_tiles, i % tiles_per_batch, tiles_per_batch), 0)

    tab = pl.BlockSpec((ROW_TILE, LANES), tab_idx)
    return pl.pallas_call(
        functools.partial(_in_proj_kernel, rows=rows),
        grid=(rows.n_tiles,),
        in_specs=rows.specs() + [pl.BlockSpec((1, 6, d), mod_idx), _const_spec((1, d)),
                                 _layer_spec(w, layer), tab, tab, tab],
        out_specs=[pl.BlockSpec((ROW_TILE, PA_W), lambda i: (i, 0)),
                   pl.BlockSpec((ROW_TILE, PB_W), lambda i: (i, 0)),
                   pl.BlockSpec((ROW_TILE, PC_W), lambda i: (i, 0))],
        out_shape=[jax.ShapeDtypeStruct((n_rows, PA_W), BF16),
                   jax.ShapeDtypeStruct((n_rows, PB_W), F32),
                   jax.ShapeDtypeStruct((n_rows, PC_W), BF16)],
        compiler_params=_params("arbitrary"),
        name="in_proj",
    )(*rows.arrays, mods, g, w, cos, sup, sdn)


def _diff_attn_kernel(*refs, lam_init, need_ctx):
    if need_ctx:
        (q_ref, kx_ref, vx_ref, kc_ref, vc_ref, qc_ref, lam_ref, g_ref, o_ref, oc_ref,
         vxt_ref, vct_ref, s_ref, acc_ref) = refs
    else:
        (q_ref, kx_ref, vx_ref, kc_ref, vc_ref, lam_ref, g_ref, o_ref,
         vxt_ref, vct_ref, s_ref, acc_ref) = refs
    seq, n_ctx = kx_ref.shape[0], kc_ref.shape[0]
    KC = n_ctx
    n_tiles = seq // Q_SUB
    lv = lam_ref[...]
    lam = (jnp.exp(jnp.sum(lv[0:1] * lv[1:2], axis=-1, keepdims=True))
           - jnp.exp(jnp.sum(lv[2:3] * lv[3:4], axis=-1, keepdims=True)) + lam_init)
    gain = g_ref[...] * (1.0 - lam_init)

    def fold8(a, op):
        return op(a.reshape(a.shape[0] // 8, 8, a.shape[1]), axis=0)

    def scores(slot, qs, k_blk, row0, m):
        out = []
        for c in range(2):
            s = _dot_nt(k_blk, qs[c])
            s_ref[2 * slot + c, row0:row0 + KC, :] = s
            part = fold8(s, jnp.max)
            out.append(part if m is None else jnp.maximum(m[c], part))
        return tuple(out)

    def weigh(slot, vt_blk, row0, m, den):
        out = []
        for c in range(2):
            p = jnp.exp2(s_ref[2 * slot + c, row0:row0 + KC, :] - m[c])
            part = fold8(p, jnp.sum)
            out.append(part if den is None else den[c] + part)
            t = _dot(vt_blk, p.astype(BF16))
            if den is None:
                acc_ref[c] = t
            else:
                acc_ref[c] += t
        return tuple(out)

    def finish(den, dst_ref, r0):
        inv0 = 1.0 / den[0].sum(axis=0, keepdims=True)
        inv1 = lam / den[1].sum(axis=0, keepdims=True)
        o = acc_ref[0] * inv0 - acc_ref[1] * inv1
        ms = jnp.mean(o * o, axis=0, keepdims=True)
        o = o * lax.rsqrt(ms + NORM_EPS) * gain
        dst_ref[pl.ds(r0, Q_SUB), :] = o.T.astype(dst_ref.dtype)

    def col_max(m):
        return tuple(t.max(axis=0, keepdims=True) for t in m)

    chunks = [(lambda: kc_ref[...], lambda: vct_ref[...], seq)]
    for i in range(seq // KC):
        chunks.append((lambda i=i: kx_ref[i * KC:(i + 1) * KC, :],
                       lambda i=i: vxt_ref[:, i * KC:(i + 1) * KC], i * KC))

    def tile_q(u):
        return _split_heads(q_ref[pl.ds(pl.multiple_of(u * Q_SUB, Q_SUB), Q_SUB), :])

    vxt_ref[...] = vx_ref[...].T
    vct_ref[...] = vc_ref[...].T
    if need_ctx:
        k_of, vt_of, row0 = chunks[0]
        m = col_max(scores(1, _split_heads(qc_ref[...]), k_of(), row0, None))
        finish(weigh(1, vt_of(), row0, m, None), oc_ref, 0)

    m0 = None
    q0 = tile_q(0)
    for k_of, _, row0 in chunks:
        m0 = scores(0, q0, k_of(), row0, m0)

    def step(u, slot, m_cur):
        q_next = tile_q(u + 1)
        m_next, den = None, None
        for k_of, vt_of, row0 in chunks:
            m_next = scores(1 - slot, q_next, k_of(), row0, m_next)
            den = weigh(slot, vt_of(), row0, m_cur, den)
        finish(den, o_ref, pl.multiple_of(u * Q_SUB, Q_SUB))
        return col_max(m_next)

    assert n_tiles % 2 == 0
    m_cur = lax.fori_loop(0, n_tiles // 2 - 1,
                          lambda i, m: step(2 * i + 1, 1, step(2 * i, 0, m)), col_max(m0))
    m_last = step(n_tiles - 2, 0, m_cur)
    den = None
    for _, vt_of, row0 in chunks:
        den = weigh(1, vt_of(), row0, m_last, den)
    finish(den, o_ref, (n_tiles - 1) * Q_SUB)


def _diff_attn(pa, lamv, g, lam_init, batch, seq, ctx_len, need_ctx):
    ctx_k0 = batch * seq // ctx_len

    def lat(col):
        return pl.BlockSpec((seq, LANES), lambda b, h: (b, col + h))

    def ctx(col):
        return pl.BlockSpec((ctx_len, LANES), lambda b, h: (ctx_k0 + b, col + h))

    in_specs = [lat(0), lat(N_A), lat(2 * N_A), ctx(N_A), ctx(2 * N_A)]
    args = [pa, pa, pa, pa, pa]
    out_specs = [pl.BlockSpec((seq, LANES), lambda b, h: (b, h))]
    out_shape = [jax.ShapeDtypeStruct((batch * seq, N_A * LANES), BF16)]
    if need_ctx:
        in_specs.append(ctx(0))
        args.append(pa)
        out_specs.append(pl.BlockSpec((ctx_len, LANES), lambda b, h: (b, h)))
        out_shape.append(jax.ShapeDtypeStruct((batch * ctx_len, N_A * LANES), BF16))
    in_specs += [_const_spec((4, HEAD_DIM)), _const_spec((LANES, 1))]
    args += [lamv, g]
    out = pl.pallas_call(
        functools.partial(_diff_attn_kernel, lam_init=lam_init, need_ctx=need_ctx),
        grid=(batch, N_A),
        in_specs=in_specs,
        out_specs=out_specs,
        out_shape=out_shape,
        scratch_shapes=[pltpu.VMEM((LANES, seq), BF16), pltpu.VMEM((LANES, ctx_len), BF16),
                        pltpu.VMEM((4, seq + ctx_len, Q_SUB), F32),
                        pltpu.VMEM((2, LANES, Q_SUB), F32)],
        compiler_params=_params("arbitrary", "arbitrary"),
        name="diff_attn",
    )(*args)
    return out[0], (out[1] if need_ctx else None)


def _retention_kernel(*refs, need_ctx):
    if need_ctx:
        (q_ref, k_ref, v_ref, g_ref, qc_ref, kc_ref, vc_ref, gc_ref, df_ref, db_ref, o_ref, oc_ref,
         u_ref, st_ref) = refs
    else:
        (q_ref, k_ref, v_ref, g_ref, kc_ref, vc_ref, df_ref, db_ref, o_ref, u_ref, st_ref) = refs
    C = RET_CHUNK
    lgf = jax.nn.log_sigmoid(df_ref[0])
    lgb = jax.nn.log_sigmoid(db_ref[0])
    row = lax.broadcasted_iota(jnp.int32, (C, LANES), 0).astype(F32)
    qdec = jnp.concatenate([jnp.exp((row + 1.0) * lgf), jnp.exp((C - row) * lgb)], axis=1)
    kdec_f = jnp.exp((C - 1.0 - row) * lgf)
    kdec_b = jnp.exp(row * lgb)
    cdec_f = jnp.exp(C * lgf)
    cdec_b = jnp.exp(C * lgb)
    ii = lax.broadcasted_iota(jnp.int32, (C, C), 0)
    mm = lax.broadcasted_iota(jnp.int32, (C, C), 1)
    dist = (ii - mm).astype(F32)
    causal = ii >= mm
    fwd_d = jnp.where(causal, dist, 0.0)
    bwd_d = jnp.where(causal, 0.0, -dist)
    intra = jnp.concatenate(
        [jnp.where(causal, jnp.exp(fwd_d * lgf[:, c:c + 1]), jnp.exp(bwd_d * lgb[:, c:c + 1]))
         for c in (0, HEAD_DIM)], axis=0)
    lo = _lane_lo((C, LANES))
    same_head = lo == (lax.broadcasted_iota(jnp.int32, (C, LANES), 0) < HEAD_DIM)

    def state_of(k, v):
        return jnp.where(same_head, _dot_tn(k.astype(BF16), v.astype(BF16)), 0.0)

    def sweep(q_ref, k_ref, v_ref, g_ref, o_ref, f0, g0):
        n = q_ref.shape[0] // C
        for j in range(n):
            k, v = k_ref[j * C:(j + 1) * C, :], v_ref[j * C:(j + 1) * C, :]
            u_ref[0, j] = state_of(k * kdec_f, v)
            u_ref[1, j] = state_of(k * kdec_b, v)
        s = f0
        for j in range(n):
            st_ref[j, 0:LANES, :] = s.astype(BF16)
            if j + 1 < n:
                s = cdec_f * s + u_ref[0, j]
        s = g0
        for j in reversed(range(n)):
            st_ref[j, LANES:2 * LANES, :] = s.astype(BF16)
            if j > 0:
                s = cdec_b * s + u_ref[1, j]
        for j in range(n):
            sl = slice(j * C, (j + 1) * C)
            q, k, v, g = q_ref[sl, :], k_ref[sl, :], v_ref[sl, :], g_ref[sl, :]
            q2 = jnp.concatenate([q, q], axis=1) * qdec
            y = _dot(q2.astype(BF16), st_ref[j])
            qh = jnp.concatenate(_split_heads(q.astype(BF16)), axis=0)
            s = _dot_nt(qh, k.astype(BF16)) * intra
            inner = _dot(s.astype(BF16), v.astype(BF16))
            y = y + jnp.where(lo, inner[:C], inner[C:])
            ylo = jnp.where(lo, y, 0.0)
            s_lo = ylo.sum(axis=-1, keepdims=True)
            s_hi = (y - ylo).sum(axis=-1, keepdims=True)
            yc = y - jnp.where(lo, s_lo, s_hi) * (1.0 / HEAD_DIM)
            sq = yc * yc
            sqlo = jnp.where(lo, sq, 0.0)
            v_lo = sqlo.sum(axis=-1, keepdims=True)
            v_hi = (sq - sqlo).sum(axis=-1, keepdims=True)
            var = jnp.where(lo, v_lo, v_hi) * (1.0 / HEAD_DIM)
            yn = yc * lax.rsqrt(var + NORM_EPS)
            o_ref[sl, :] = (g * jax.nn.sigmoid(g) * yn).astype(o_ref.dtype)

    n_ctx = kc_ref.shape[0]
    pos = lax.broadcasted_iota(jnp.int32, (n_ctx, LANES), 0).astype(F32)
    kc, vc = kc_ref[...], vc_ref[...]
    f0 = state_of(kc * jnp.exp((n_ctx - 1.0 - pos) * lgf), vc)
    g0 = state_of(kc * jnp.exp(pos * lgb), vc)
    sweep(q_ref, k_ref, v_ref, g_ref, o_ref, f0, g0)
    if need_ctx:
        zero = jnp.zeros((LANES, LANES), F32)
        sweep(qc_ref, kc_ref, vc_ref, gc_ref, oc_ref, zero, zero)


def _retention(pb, dec_f, dec_b, batch, seq, ctx_len, need_ctx):
    ctx_blk0 = batch * seq // ctx_len

    def lat(col):
        return pl.BlockSpec((seq, LANES), lambda b, p: (b, col + p))

    def ctx(col):
        return pl.BlockSpec((ctx_len, LANES), lambda b, p: (ctx_blk0 + b, col + p))

    dec = pl.BlockSpec((1, 1, LANES), lambda b, p: (p, 0, 0))
    in_specs = [lat(0), lat(2), lat(4), lat(6)]
    in_specs += [ctx(0), ctx(2), ctx(4), ctx(6)] if need_ctx else [ctx(2), ctx(4)]
    in_specs += [dec, dec]
    args = [pb] * (len(in_specs) - 2) + [dec_f, dec_b]
    out_specs = [pl.BlockSpec((seq, LANES), lambda b, p: (b, p))]
    out_shape = [jax.ShapeDtypeStruct((batch * seq, 2 * LANES), BF16)]
    if need_ctx:
        out_specs.append(pl.BlockSpec((ctx_len, LANES), lambda b, p: (b, p)))
        out_shape.append(jax.ShapeDtypeStruct((batch * ctx_len, 2 * LANES), BF16))
    n_chunks = seq // RET_CHUNK
    out = pl.pallas_call(
        functools.partial(_retention_kernel, need_ctx=need_ctx),
        grid=(batch, 2),
        in_specs=in_specs,
        out_specs=out_specs,
        out_shape=out_shape,
        scratch_shapes=[pltpu.VMEM((2, n_chunks, LANES, LANES), F32),
                        pltpu.VMEM((n_chunks, 2 * LANES, LANES), BF16)],
        compiler_params=_params("arbitrary", "arbitrary"),
        name="retention",
    )(*args)
    return out[0], (out[1] if need_ctx else None)


def _window_attn_kernel(*refs, need_ctx):
    if need_ctx:
        (q_ref, k_ref, v_ref, kc_ref, vc_ref, qc_ref, sink_ref, o_ref, oc_ref,
         vt_ref, vct_ref, sw_ref, sc_ref) = refs
    else:
        q_ref, k_ref, v_ref, kc_ref, vc_ref, sink_ref, o_ref, vt_ref, vct_ref, sw_ref, sc_ref = refs
    W = WINDOW
    n_blocks = q_ref.shape[0] // W
    vt_ref[...] = v_ref[...].T
    vct_ref[...] = vc_ref[...].T
    sink = sink_ref[0]
    kj = lax.broadcasted_iota(jnp.int32, (W, 2 * W), 0)
    qi = lax.broadcasted_iota(jnp.int32, (W, 2 * W), 1) % W
    keep = {"prev": kj >= qi, "cur": kj >= 0, "next": kj <= qi}
    upper = lax.broadcasted_iota(jnp.int32, (LANES, W), 0) < HEAD_DIM

    def scores(slot, q, win, kinds):
        qs = jnp.concatenate(_split_heads(q), axis=0)
        s_ctx = _dot_nt(kc_ref[...], qs)
        sc_ref[slot] = s_ctx
        m = jnp.maximum(s_ctx.max(axis=0, keepdims=True), sink)
        if win is not None:
            mask = jnp.concatenate([keep[kind] for kind in kinds], axis=0)
            s_win = jnp.where(mask, _dot_nt(k_ref[win, :], qs), -jnp.inf)
            sw_ref[slot, 0:s_win.shape[0], :] = s_win
            m = jnp.maximum(m, s_win.max(axis=0, keepdims=True))
        return m

    def weigh(slot, m, win, dst_ref, r0):
        p_ctx = jnp.exp(sc_ref[slot] - m)
        den = p_ctx.sum(axis=0, keepdims=True) + jnp.exp(sink - m)
        o = _dot(vct_ref[...], p_ctx.astype(BF16))
        if win is not None:
            p_win = jnp.exp(sw_ref[slot, 0:win.stop - win.start, :] - m)
            den = den + p_win.sum(axis=0, keepdims=True)
            o = o + _dot(vt_ref[:, win], p_win.astype(BF16))
        o = o * (1.0 / den)
        o = jnp.where(upper, o[:, :W], o[:, W:])
        dst_ref[r0:r0 + W, :] = o.T.astype(dst_ref.dtype)

    work = []
    for n in range(n_blocks):
        first, last = max(n - 1, 0), min(n + 1, n_blocks - 1)
        kinds = (["prev"] if n > 0 else []) + ["cur"] + (["next"] if n < n_blocks - 1 else [])
        work.append((q_ref, slice(first * W, (last + 1) * W), kinds, o_ref, n * W))
    if need_ctx:
        for n in range(qc_ref.shape[0] // W):
            work.append((qc_ref, None, None, oc_ref, n * W))

    m_prev = None
    for i, (src_ref, win, kinds, dst_ref, r0) in enumerate(work):
        m = scores(i % 2, src_ref[r0:r0 + W, :], win, kinds)
        if i > 0:
            _, pwin, _, pdst, pr0 = work[i - 1]
            weigh((i - 1) % 2, m_prev, pwin, pdst, pr0)
        m_prev = m
    _, pwin, _, pdst, pr0 = work[-1]
    weigh((len(work) - 1) % 2, m_prev, pwin, pdst, pr0)


def _window_attn(pc, sink, batch, seq, ctx_len, need_ctx):
    ctx_k0 = batch * seq // ctx_len

    def lat(col):
        return pl.BlockSpec((seq, LANES), lambda b, h: (b, col + h))

    def ctx(col):
        return pl.BlockSpec((ctx_len, LANES), lambda b, h: (ctx_k0 + b, col + h))

    in_specs = [lat(0), lat(2), lat(4), ctx(2), ctx(4)]
    out_specs = [pl.BlockSpec((seq, LANES), lambda b, h: (b, h))]
    out_shape = [jax.ShapeDtypeStruct((batch * seq, 2 * LANES), BF16)]
    if need_ctx:
        in_specs.append(ctx(0))
        out_specs.append(pl.BlockSpec((ctx_len, LANES), lambda b, h: (b, h)))
        out_shape.append(jax.ShapeDtypeStruct((batch * ctx_len, 2 * LANES), BF16))
    args = [pc] * len(in_specs) + [sink]
    in_specs.append(pl.BlockSpec((1, 1, 2 * WINDOW), lambda b, h: (h, 0, 0)))
    out = pl.pallas_call(
        functools.partial(_window_attn_kernel, need_ctx=need_ctx),
        grid=(batch, 2),
        in_specs=in_specs,
        out_specs=out_specs,
        out_shape=out_shape,
        scratch_shapes=[pltpu.VMEM((LANES, seq), BF16), pltpu.VMEM((LANES, ctx_len), BF16),
                        pltpu.VMEM((2, 3 * WINDOW, 2 * WINDOW), F32),
                        pltpu.VMEM((2, ctx_len, 2 * WINDOW), F32)],
        compiler_params=_params("arbitrary", "arbitrary"),
        name="window_attn",
    )(*args)
    return out[0], (out[1] if need_ctx else None)


def _out_mlp_kernel(*refs, rows, has_ctx, final_norm):
    n_x = len(rows.arrays)
    n_mix = 6 if has_ctx else 3
    x_refs = refs[:n_x]
    mix_refs = refs[n_x:n_x + n_mix]
    mod_ref, g_ref, wo_ref, w1_ref, w2_ref = refs[n_x + n_mix:n_x + n_mix + 5]
    rest = refs[n_x + n_mix + 5:]
    if final_norm:
        gf_ref, o_ref = rest
    else:
        (o_ref,) = rest
    mod = mod_ref[0]
    is_lat = pl.program_id(0) < rows.n_lat_tiles
    mix, r0 = None, 0
    for i in range(3):
        a = mix_refs[i][...]
        if has_ctx:
            a = jnp.where(is_lat, a, mix_refs[3 + i][...])
        t = _dot(a, wo_ref[r0:r0 + a.shape[1], :])
        mix = t if mix is None else mix + t
        r0 += a.shape[1]
    x1 = rows.load(x_refs) + mod[2:3] * mix
    h = _modulated_norm(x1, g_ref[...], mod[3:4], mod[4:5]).astype(BF16)
    n_ff = w1_ref.shape[1]
    step = D_MODEL
    acc = None
    for c0 in range(0, n_ff, step):
        a = jnp.maximum(_dot(h, w1_ref[:, c0:c0 + step]), 0.0)
        t = _dot((a * a).astype(BF16), w2_ref[c0:c0 + step, :])
        acc = t if acc is None else acc + t
    x2 = x1 + mod[5:6] * acc
    if final_norm:
        ms = jnp.mean(x2 * x2, axis=-1, keepdims=True)
        x2 = x2 * lax.rsqrt(ms + NORM_EPS) * gf_ref[...]
    o_ref[...] = x2


def _out_mlp(rows, lat_mix, ctx_mix, mods, g, wo, w1, w2, layer, g_final, tiles_per_batch, n_ctx_mod):
    d = rows.arrays[0].shape[1]
    n_lat_tiles = rows.n_lat_tiles
    has_ctx = ctx_mix is not None
    final_norm = g_final is not None
    n_tiles = rows.n_tiles if has_ctx else n_lat_tiles

    def mod_idx(i):
        return (jnp.where(i < n_lat_tiles, i // tiles_per_batch, n_ctx_mod), 0, 0)

    in_specs = rows.specs()
    args = list(rows.arrays)
    for a in lat_mix:
        in_specs.append(pl.BlockSpec((ROW_TILE, a.shape[1]), lambda i: (jnp.minimum(i, n_lat_tiles - 1), 0)))
        args.append(a)
    if has_ctx:
        for a in ctx_mix:
            in_specs.append(pl.BlockSpec((ROW_TILE, a.shape[1]),
                                         lambda i: (jnp.maximum(i - n_lat_tiles, 0), 0)))
            args.append(a)
    in_specs += [pl.BlockSpec((1, 6, d), mod_idx), _const_spec((1, d)),
                 _layer_spec(wo, layer), _layer_spec(w1, layer), _layer_spec(w2, layer)]
    args += [mods, g, wo, w1, w2]
    if final_norm:
        in_specs.append(_const_spec((1, d)))
        args.append(g_final)
    return pl.pallas_call(
        functools.partial(_out_mlp_kernel, rows=rows, has_ctx=has_ctx, final_norm=final_norm),
        grid=(n_tiles,),
        in_specs=in_specs,
        out_specs=pl.BlockSpec((ROW_TILE, d), lambda i: (i, 0)),
        out_shape=jax.ShapeDtypeStruct((n_tiles * ROW_TILE, d), F32),
        compiler_params=_params("arbitrary"),
        name="out_mlp",
    )(*args)


def _rope_tables(seq):
    rows = seq // GRID_W
    r = jnp.repeat(jnp.arange(rows, dtype=F32), GRID_W)
    col = jnp.tile(jnp.arange(GRID_W, dtype=F32), rows)
    nf = HEAD_DIM // 4
    inv = ROPE_BASE ** (-jnp.arange(nf, dtype=F32) / nf)
    ar, ac = r[:, None] * inv, col[:, None] * inv
    ang = jnp.concatenate([ar, ar, ac, ac], axis=-1)
    ang = jnp.concatenate([ang, ang], axis=-1)
    cos, sin = jnp.cos(ang), jnp.sin(ang)
    up = (jnp.arange(LANES) // nf) % 2 == 0
    sin_up = jnp.where(up, -sin, 0.0)
    sin_dn = jnp.where(up, 0.0, sin)
    ident = jnp.zeros((ROW_TILE, LANES), F32)
    return (jnp.concatenate([cos, ident + 1.0]), jnp.concatenate([sin_up, ident]),
            jnp.concatenate([sin_dn, ident]))


def kernel(x, c, ctx, c_ctx, w_ada, b_ada, g_mix, g_mlp, w_in, w_out, lam_q1, lam_k1, lam_q2, lam_k2,
           subln_g, ret_decay_fwd, ret_decay_bwd, sink_logit, w_mlp1, w_mlp2, g_final):
    batch, seq, d = x.shape
    ctx_len = ctx.shape[1]
    depth = w_ada.shape[0]
    assert d == D_MODEL and seq % ROW_TILE == 0 and (batch * ctx_len) % ROW_TILE == 0
    assert ctx_len == Q_SUB and ctx_len % WINDOW == 0 and seq % GRID_W == 0
    n_lat_tiles = batch * seq // ROW_TILE
    tiles_per_batch = seq // ROW_TILE

    mod_rows = 16
    cc = jnp.concatenate([c, c_ctx[None], jnp.zeros((mod_rows - batch - 1, d), F32)], axis=0)
    mods = _ada_params(cc, w_ada, b_ada).reshape(depth, mod_rows, 6, d)
    cos, sup, sdn = _rope_tables(seq)
    w_in, w_out, w_mlp1, w_mlp2 = (w.astype(BF16) for w in (w_in, w_out, w_mlp1, w_mlp2))

    rows = _Rows((x.reshape(batch * seq, d), ctx.reshape(batch * ctx_len, d)), n_lat_tiles)
    for l in range(depth):
        need_ctx = l < depth - 1
        last = l == depth - 1
        lam_init = 0.8 - 0.6 * math.exp(-0.3 * l)
        pa, pb, pc = _in_proj(rows, mods[l], g_mix[l][None], w_in, l, cos, sup, sdn,
                              tiles_per_batch, batch)
        lamv = jnp.stack([lam_q1[l], lam_k1[l], lam_q2[l], lam_k2[l]])
        oa, oa_c = _diff_attn(pa, lamv, subln_g[l][:, None], lam_init, batch, seq, ctx_len, need_ctx)
        dec_f = jnp.repeat(ret_decay_fwd[l], HEAD_DIM).reshape(2, 1, LANES)
        dec_b = jnp.repeat(ret_decay_bwd[l], HEAD_DIM).reshape(2, 1, LANES)
        ob, ob_c = _retention(pb, dec_f, dec_b, batch, seq, ctx_len, need_ctx)
        sink = jnp.repeat(sink_logit[l], WINDOW).reshape(2, 1, 2 * WINDOW)
        oc, oc_c = _window_attn(pc, sink, batch, seq, ctx_len, need_ctx)
        xs = _out_mlp(rows, (oa, ob, oc), (oa_c, ob_c, oc_c) if need_ctx else None, mods[l],
                      g_mlp[l][None], w_out, w_mlp1, w_mlp2, l, g_final[None] if last else None,
                      tiles_per_batch, batch)
        rows = _Rows((xs,), n_lat_tiles)
    return xs.reshape(batch, seq, d)
```

```python
import functools
import math

import jax
import jax.numpy as jnp
from jax import lax
from jax.experimental import pallas as pl
from jax.experimental.pallas import tpu as pltpu

D_MODEL = 1024
HEAD_DIM = 64
LANES = 128
GRID_W = 64
N_A = 4
RET_CHUNK = 128
WINDOW = 128
D_FF = 4 * D_MODEL
ROPE_BASE = 10000.0
NORM_EPS = 1e-6
IN_TILE = 1024
ROW_TILE = 512
Q_SUB = 256
VMEM_LIMIT = 56 * 1024 * 1024
LOG2E = math.log2(math.e)

PA_W = 3 * N_A * 2 * HEAD_DIM
PB_W = 4 * 4 * HEAD_DIM
PC_W = 6 * LANES

BF16 = jnp.bfloat16
F32 = jnp.float32


def _dot(a, b):
    return jnp.dot(a, b, preferred_element_type=F32)


def _dot_nt(a, b):
    return lax.dot_general(a, b, (((1,), (1,)), ((), ())), preferred_element_type=F32)


def _dot_tn(a, b):
    return lax.dot_general(a, b, (((0,), (0,)), ((), ())), preferred_element_type=F32)


def _lane_lo(shape):
    return lax.broadcasted_iota(jnp.int32, shape, len(shape) - 1) < HEAD_DIM


def _split_heads(a):
    lo = _lane_lo(a.shape)
    zero = jnp.zeros_like(a)
    return jnp.where(lo, a, zero), jnp.where(lo, zero, a)


def _params(*sem):
    return pltpu.CompilerParams(dimension_semantics=sem, vmem_limit_bytes=VMEM_LIMIT)


def _const_spec(shape):
    nd = len(shape)
    return pl.BlockSpec(shape, lambda *_: (0,) * nd, pipeline_mode=pl.Buffered(1))


def _layer_spec(arr, layer):
    tail = (0,) * (arr.ndim - 1)
    return pl.BlockSpec((None,) + arr.shape[1:], lambda *_: (layer,) + tail,
                        pipeline_mode=pl.Buffered(1))


def _ada_kernel(cc_ref, w_ref, b_ref, o_ref):
    cc = cc_ref[...]
    a = (cc * jax.nn.sigmoid(cc)).astype(BF16)
    o_ref[0] = _dot(a, w_ref[0].astype(BF16)) + b_ref[0]


def _ada_params(cc, w_ada, b_ada):
    depth, d, n6 = w_ada.shape
    tn = n6 // 4
    rows = cc.shape[0]
    return pl.pallas_call(
        _ada_kernel,
        grid=(depth, n6 // tn),
        in_specs=[pl.BlockSpec((rows, d), lambda l, j: (0, 0)),
                  pl.BlockSpec((1, d, tn), lambda l, j: (l, 0, j)),
                  pl.BlockSpec((1, 1, tn), lambda l, j: (l, 0, j))],
        out_specs=pl.BlockSpec((1, rows, tn), lambda l, j: (l, 0, j)),
        out_shape=jax.ShapeDtypeStruct((depth, rows, n6), F32),
        compiler_params=_params("arbitrary", "arbitrary"),
        name="ada_params",
    )(cc, w_ada, b_ada.reshape(depth, 1, n6))


def _modulated_norm(x, g, shift, scale):
    ms = jnp.mean(x * x, axis=-1, keepdims=True)
    return (x * lax.rsqrt(ms + NORM_EPS) * g) * (1.0 + scale) + shift


def _rope(a, cos, sin_up, sin_dn):
    return (a * cos + pltpu.roll(a, LANES - 16, axis=1) * sin_up
            + pltpu.roll(a, 16, axis=1) * sin_dn)


class _Rows:
    def __init__(self, arrays, n_lat_rows, seq, tile):
        self.arrays = arrays
        self.tile = tile
        self.n_lat_tiles = n_lat_rows // tile
        self.tiles_per_batch = seq // tile
        self.n_tiles = sum(a.shape[0] for a in arrays) // tile

    def specs(self):
        d = self.arrays[0].shape[1]
        if len(self.arrays) == 1:
            return [pl.BlockSpec((self.tile, d), lambda i: (i, 0))]
        n = self.n_lat_tiles
        return [pl.BlockSpec((self.tile, d), lambda i: (jnp.minimum(i, n - 1), 0)),
                pl.BlockSpec((self.tile, d), lambda i: (jnp.maximum(i - n, 0), 0))]

    def mod_spec(self, n_ctx_mod):
        d = self.arrays[0].shape[1]
        n, per = self.n_lat_tiles, self.tiles_per_batch
        return pl.BlockSpec((1, 6, d), lambda i: (jnp.where(i < n, i // per, n_ctx_mod), 0, 0))

    def load(self, refs):
        if len(refs) == 1:
            return refs[0][...]
        return jnp.where(pl.program_id(0) < self.n_lat_tiles, refs[0][...], refs[1][...])


_Q_SCALE = HEAD_DIM ** -0.5 * LOG2E
_A_W = N_A * 2 * HEAD_DIM
_SEGS_A = ((0, 4, True, _Q_SCALE), (_A_W, 4, True, 1.0), (2 * _A_W, 4, False, 1.0))
_B0 = 3 * _A_W
_SEGS_B = ((_B0, 2, True, 1.0), (_B0 + 256, 2, True, HEAD_DIM ** -0.5),
           (_B0 + 512, 2, False, 1.0), (_B0 + 768, 2, False, 1.0))
_C0 = _B0 + 1024


def _in_proj_kernel(*refs, rows):
    n_x = len(rows.arrays)
    x_refs = refs[:n_x]
    mod_ref, g_ref, w_ref, cos_ref, sup_ref, sdn_ref, pa_ref, pb_ref, pc_ref = refs[n_x:]
    mod = mod_ref[0]
    h = _modulated_norm(rows.load(x_refs), g_ref[...], mod[0:1], mod[1:2]).astype(BF16)
    cos, sup, sdn = cos_ref[...], sup_ref[...], sdn_ref[...]

    def project(col0, nblk):
        return _dot(h, w_ref[:, col0:col0 + nblk * LANES])

    def finish(acc, j, rope, mult):
        a = acc[:, j * LANES:(j + 1) * LANES]
        if rope:
            a = _rope(a, cos, sup, sdn)
        if mult != 1.0:
            a = a * mult
        return a

    acc = project(_C0, 4)
    lo = _lane_lo((acc.shape[0], LANES))
    for j in range(2):
        pc_ref[:, j * LANES:(j + 1) * LANES] = finish(acc, j, True, _Q_SCALE).astype(BF16)
    for j, rope in ((2, True), (3, False)):
        a = finish(acc, j, rope, 1.0)
        r = pltpu.roll(a, HEAD_DIM, axis=1)
        base = (2 + 2 * (j - 2)) * LANES
        pc_ref[:, base:base + LANES] = jnp.where(lo, a, r).astype(BF16)
        pc_ref[:, base + LANES:base + 2 * LANES] = jnp.where(lo, r, a).astype(BF16)

    work = []
    for segs, ref in ((_SEGS_A, pa_ref), (_SEGS_B, pb_ref)):
        out_col = 0
        for seg in segs:
            work.append((not seg[2], ref, out_col) + seg)
            out_col += seg[1] * LANES
    for _, ref, out_col, col0, nblk, rope, mult in sorted(work, key=lambda w: w[0]):
        acc = project(col0, nblk)
        for j in range(nblk):
            ref[:, out_col + j * LANES:out_col + (j + 1) * LANES] = (
                finish(acc, j, rope, mult).astype(ref.dtype))


def _in_proj(rows, mods, g, w, layer, cos, sup, sdn, n_ctx_mod):
    d = rows.arrays[0].shape[1]
    n_lat_tiles, tiles_per_batch, tile = rows.n_lat_tiles, rows.tiles_per_batch, rows.tile
    n_rows = rows.n_tiles * tile

    def tab_idx(i):
        return (jnp.where(i < n_lat_tiles, i % tiles_per_batch, tiles_per_batch), 0)

    tab = pl.BlockSpec((tile, LANES), tab_idx)
    return pl.pallas_call(
        functools.partial(_in_proj_kernel, rows=rows),
        grid=(rows.n_tiles,),
        in_specs=rows.specs() + [rows.mod_spec(n_ctx_mod), _const_spec((1, d)),
                                 _layer_spec(w, layer), tab, tab, tab],
        out_specs=[pl.BlockSpec((tile, PA_W), lambda i: (i, 0)),
                   pl.BlockSpec((tile, PB_W), lambda i: (i, 0)),
                   pl.BlockSpec((tile, PC_W), lambda i: (i, 0))],
        out_shape=[jax.ShapeDtypeStruct((n_rows, PA_W), BF16),
                   jax.ShapeDtypeStruct((n_rows, PB_W), F32),
                   jax.ShapeDtypeStruct((n_rows, PC_W), BF16)],
        compiler_params=_params("arbitrary"),
        name="in_proj",
    )(*rows.arrays, mods, g, w, cos, sup, sdn)


def _diff_attn_kernel(*refs, lam_init, need_ctx):
    if need_ctx:
        (q_ref, kx_ref, vx_ref, kc_ref, vc_ref, qc_ref, lam_ref, g_ref, o_ref, oc_ref,
         vxt_ref, vct_ref, s_ref, acc_ref) = refs
    else:
        (q_ref, kx_ref, vx_ref, kc_ref, vc_ref, lam_ref, g_ref, o_ref,
         vxt_ref, vct_ref, s_ref, acc_ref) = refs
    seq, n_ctx = kx_ref.shape[0], kc_ref.shape[0]
    KC = n_ctx
    n_tiles = seq // Q_SUB
    lv = lam_ref[...]
    lam = (jnp.exp(jnp.sum(lv[0:1] * lv[1:2], axis=-1, keepdims=True))
           - jnp.exp(jnp.sum(lv[2:3] * lv[3:4], axis=-1, keepdims=True)) + lam_init)
    gain = g_ref[...] * (1.0 - lam_init)

    def fold8(a, op):
        return op(a.reshape(a.shape[0] // 8, 8, a.shape[1]), axis=0)

    def scores(slot, qs, k_blk, row0, m):
        out = []
        for c in range(2):
            s = _dot_nt(k_blk, qs[c])
            s_ref[2 * slot + c, row0:row0 + KC, :] = s
            part = fold8(s, jnp.max)
            out.append(part if m is None else jnp.maximum(m[c], part))
        return tuple(out)

    def weigh(slot, vt_blk, row0, m, den):
        out = []
        for c in range(2):
            p = jnp.exp2(s_ref[2 * slot + c, row0:row0 + KC, :] - m[c])
            part = fold8(p, jnp.sum)
            out.append(part if den is None else den[c] + part)
            t = _dot(vt_blk, p.astype(BF16))
            if den is None:
                acc_ref[2 * slot + c] = t
            else:
                acc_ref[2 * slot + c] += t
        return tuple(out)

    def finish(slot, den, dst_ref, r0):
        inv0 = 1.0 / den[0].sum(axis=0, keepdims=True)
        inv1 = lam / den[1].sum(axis=0, keepdims=True)
        o = acc_ref[2 * slot] * inv0 - acc_ref[2 * slot + 1] * inv1
        ms = jnp.mean(o * o, axis=0, keepdims=True)
        o = o * lax.rsqrt(ms + NORM_EPS) * gain
        dst_ref[pl.ds(r0, Q_SUB), :] = o.T.astype(dst_ref.dtype)

    def col_max(m):
        return tuple(t.max(axis=0, keepdims=True) for t in m)

    chunks = [(lambda: kc_ref[...], lambda: vct_ref[...], seq)]
    for i in range(seq // KC):
        chunks.append((lambda i=i: kx_ref[i * KC:(i + 1) * KC, :],
                       lambda i=i: vxt_ref[:, i * KC:(i + 1) * KC], i * KC))

    def tile_q(u):
        return _split_heads(q_ref[pl.ds(pl.multiple_of(u * Q_SUB, Q_SUB), Q_SUB), :])

    vxt_ref[...] = vx_ref[...].T
    vct_ref[...] = vc_ref[...].T
    if need_ctx:
        k_of, vt_of, row0 = chunks[0]
        m = col_max(scores(1, _split_heads(qc_ref[...]), k_of(), row0, None))
        finish(1, weigh(1, vt_of(), row0, m, None), oc_ref, 0)

    m0 = None
    q0 = tile_q(0)
    for k_of, _, row0 in chunks:
        m0 = scores(0, q0, k_of(), row0, m0)

    def step(u, slot, carry):
        m_cur, den_prev = carry
        if den_prev is not None:
            finish(1 - slot, den_prev, o_ref, pl.multiple_of((u - 1) * Q_SUB, Q_SUB))
        q_next = tile_q(u + 1)
        m_next, den = None, None
        for k_of, vt_of, row0 in chunks:
            m_next = scores(1 - slot, q_next, k_of(), row0, m_next)
            den = weigh(slot, vt_of(), row0, m_cur, den)
        return col_max(m_next), den

    assert n_tiles % 2 == 0 and n_tiles >= 4
    carry = step(0, 0, (col_max(m0), None))
    carry = lax.fori_loop(0, n_tiles // 2 - 1,
                          lambda i, c: step(2 * i + 2, 0, step(2 * i + 1, 1, c)), carry)
    m_last, den_prev = carry
    finish(0, den_prev, o_ref, (n_tiles - 2) * Q_SUB)
    den = None
    for _, vt_of, row0 in chunks:
        den = weigh(1, vt_of(), row0, m_last, den)
    finish(1, den, o_ref, (n_tiles - 1) * Q_SUB)


def _diff_attn(pa, lamv, g, lam_init, batch, seq, ctx_len, need_ctx):
    ctx_k0 = batch * seq // ctx_len

    def lat(col):
        return pl.BlockSpec((seq, LANES), lambda b, h: (b, col + h))

    def ctx(col):
        return pl.BlockSpec((ctx_len, LANES), lambda b, h: (ctx_k0 + b, col + h))

    in_specs = [lat(0), lat(N_A), lat(2 * N_A), ctx(N_A), ctx(2 * N_A)]
    args = [pa, pa, pa, pa, pa]
    out_specs = [pl.BlockSpec((seq, LANES), lambda b, h: (b, h))]
    out_shape = [jax.ShapeDtypeStruct((batch * seq, N_A * LANES), BF16)]
    if need_ctx:
        in_specs.append(ctx(0))
        args.append(pa)
        out_specs.append(pl.BlockSpec((ctx_len, LANES), lambda b, h: (b, h)))
        out_shape.append(jax.ShapeDtypeStruct((batch * ctx_len, N_A * LANES), BF16))
    in_specs += [_const_spec((4, HEAD_DIM)), _const_spec((LANES, 1))]
    args += [lamv, g]
    out = pl.pallas_call(
        functools.partial(_diff_attn_kernel, lam_init=lam_init, need_ctx=need_ctx),
        grid=(batch, N_A),
        in_specs=in_specs,
        out_specs=out_specs,
        out_shape=out_shape,
        scratch_shapes=[pltpu.VMEM((LANES, seq), BF16), pltpu.VMEM((LANES, ctx_len), BF16),
                        pltpu.VMEM((4, seq + ctx_len, Q_SUB), F32),
                        pltpu.VMEM((4, LANES, Q_SUB), F32)],
        compiler_params=_params("arbitrary", "arbitrary"),
        name="diff_attn",
    )(*args)
    return out[0], (out[1] if need_ctx else None)


def _retention_kernel(*refs, need_ctx):
    if need_ctx:
        (q_ref, k_ref, v_ref, g_ref, qc_ref, kc_ref, vc_ref, gc_ref, df_ref, db_ref, o_ref, oc_ref,
         u_ref, st_ref) = refs
    else:
        (q_ref, k_ref, v_ref, g_ref, kc_ref, vc_ref, df_ref, db_ref, o_ref, u_ref, st_ref) = refs
    C = RET_CHUNK
    lgf = jax.nn.log_sigmoid(df_ref[0])
    lgb = jax.nn.log_sigmoid(db_ref[0])
    row = lax.broadcasted_iota(jnp.int32, (C, LANES), 0).astype(F32)
    qdec = jnp.concatenate([jnp.exp((row + 1.0) * lgf), jnp.exp((C - row) * lgb)], axis=1)
    kdec_f = jnp.exp((C - 1.0 - row) * lgf)
    kdec_b = jnp.exp(row * lgb)
    cdec_f = jnp.exp(C * lgf)
    cdec_b = jnp.exp(C * lgb)
    ii = lax.broadcasted_iota(jnp.int32, (C, C), 0)
    mm = lax.broadcasted_iota(jnp.int32, (C, C), 1)
    dist = (ii - mm).astype(F32)
    causal = ii >= mm
    fwd_d = jnp.where(causal, dist, 0.0)
    bwd_d = jnp.where(causal, 0.0, -dist)
    intra = jnp.concatenate(
        [jnp.where(causal, jnp.exp(fwd_d * lgf[:, c:c + 1]), jnp.exp(bwd_d * lgb[:, c:c + 1]))
         for c in (0, HEAD_DIM)], axis=0)
    lo = _lane_lo((C, LANES))
    same_head = lo == (lax.broadcasted_iota(jnp.int32, (C, LANES), 0) < HEAD_DIM)

    def state_of(k, v):
        return jnp.where(same_head, _dot_tn(k.astype(BF16), v.astype(BF16)), 0.0)

    def sweep(q_ref, k_ref, v_ref, g_ref, o_ref, f0, g0):
        n = q_ref.shape[0] // C
        for j in range(n):
            k, v = k_ref[j * C:(j + 1) * C, :], v_ref[j * C:(j + 1) * C, :]
            u_ref[0, j] = state_of(k * kdec_f, v)
            u_ref[1, j] = state_of(k * kdec_b, v)
        s = f0
        for j in range(n):
            st_ref[j, 0:LANES, :] = s.astype(BF16)
            if j + 1 < n:
                s = cdec_f * s + u_ref[0, j]
        s = g0
        for j in reversed(range(n)):
            st_ref[j, LANES:2 * LANES, :] = s.astype(BF16)
            if j > 0:
                s = cdec_b * s + u_ref[1, j]
        for j in range(n):
            sl = slice(j * C, (j + 1) * C)
            q, k, v, g = q_ref[sl, :], k_ref[sl, :], v_ref[sl, :], g_ref[sl, :]
            q2 = jnp.concatenate([q, q], axis=1) * qdec
            y = _dot(q2.astype(BF16), st_ref[j])
            qh = jnp.concatenate(_split_heads(q.astype(BF16)), axis=0)
            s = _dot_nt(qh, k.astype(BF16)) * intra
            inner = _dot(s.astype(BF16), v.astype(BF16))
            y = y + jnp.where(lo, inner[:C], inner[C:])
            ylo = jnp.where(lo, y, 0.0)
            s_lo = ylo.sum(axis=-1, keepdims=True)
            s_hi = (y - ylo).sum(axis=-1, keepdims=True)
            yc = y - jnp.where(lo, s_lo, s_hi) * (1.0 / HEAD_DIM)
            sq = yc * yc
            sqlo = jnp.where(lo, sq, 0.0)
            v_lo = sqlo.sum(axis=-1, keepdims=True)
            v_hi = (sq - sqlo).sum(axis=-1, keepdims=True)
            var = jnp.where(lo, v_lo, v_hi) * (1.0 / HEAD_DIM)
            yn = yc * lax.rsqrt(var + NORM_EPS)
            o_ref[sl, :] = (g * jax.nn.sigmoid(g) * yn).astype(o_ref.dtype)

    n_ctx = kc_ref.shape[0]
    pos = lax.broadcasted_iota(jnp.int32, (n_ctx, LANES), 0).astype(F32)
    kc, vc = kc_ref[...], vc_ref[...]
    f0 = state_of(kc * jnp.exp((n_ctx - 1.0 - pos) * lgf), vc)
    g0 = state_of(kc * jnp.exp(pos * lgb), vc)
    sweep(q_ref, k_ref, v_ref, g_ref, o_ref, f0, g0)
    if need_ctx:
        zero = jnp.zeros((LANES, LANES), F32)
        sweep(qc_ref, kc_ref, vc_ref, gc_ref, oc_ref, zero, zero)


def _retention(pb, dec_f, dec_b, batch, seq, ctx_len, need_ctx):
    ctx_blk0 = batch * seq // ctx_len

    def lat(col):
        return pl.BlockSpec((seq, LANES), lambda b, p: (b, col + p))

    def ctx(col):
        return pl.BlockSpec((ctx_len, LANES), lambda b, p: (ctx_blk0 + b, col + p))

    dec = pl.BlockSpec((1, 1, LANES), lambda b, p: (p, 0, 0))
    in_specs = [lat(0), lat(2), lat(4), lat(6)]
    in_specs += [ctx(0), ctx(2), ctx(4), ctx(6)] if need_ctx else [ctx(2), ctx(4)]
    in_specs += [dec, dec]
    args = [pb] * (len(in_specs) - 2) + [dec_f, dec_b]
    out_specs = [pl.BlockSpec((seq, LANES), lambda b, p: (b, p))]
    out_shape = [jax.ShapeDtypeStruct((batch * seq, 2 * LANES), BF16)]
    if need_ctx:
        out_specs.append(pl.BlockSpec((ctx_len, LANES), lambda b, p: (b, p)))
        out_shape.append(jax.ShapeDtypeStruct((batch * ctx_len, 2 * LANES), BF16))
    n_chunks = seq // RET_CHUNK
    out = pl.pallas_call(
        functools.partial(_retention_kernel, need_ctx=need_ctx),
        grid=(batch, 2),
        in_specs=in_specs,
        out_specs=out_specs,
        out_shape=out_shape,
        scratch_shapes=[pltpu.VMEM((2, n_chunks, LANES, LANES), F32),
                        pltpu.VMEM((n_chunks, 2 * LANES, LANES), BF16)],
        compiler_params=_params("arbitrary", "arbitrary"),
        name="retention",
    )(*args)
    return out[0], (out[1] if need_ctx else None)


def _window_attn_kernel(*refs, need_ctx):
    if need_ctx:
        (q_ref, k_ref, v_ref, kc_ref, vc_ref, qc_ref, sink_ref, o_ref, oc_ref,
         vt_ref, vct_ref, sw_ref, sc_ref) = refs
    else:
        q_ref, k_ref, v_ref, kc_ref, vc_ref, sink_ref, o_ref, vt_ref, vct_ref, sw_ref, sc_ref = refs
    W = WINDOW
    n_blocks = q_ref.shape[0] // W
    vt_ref[...] = v_ref[...].T
    vct_ref[...] = vc_ref[...].T
    sink = sink_ref[0] * LOG2E
    kj = lax.broadcasted_iota(jnp.int32, (W, 2 * W), 0)
    qi = lax.broadcasted_iota(jnp.int32, (W, 2 * W), 1) % W
    keep = {"prev": kj >= qi, "cur": kj >= 0, "next": kj <= qi}
    upper = lax.broadcasted_iota(jnp.int32, (LANES, W), 0) < HEAD_DIM

    def scores(slot, q, win, kinds):
        qs = jnp.concatenate(_split_heads(q), axis=0)
        s_ctx = _dot_nt(kc_ref[...], qs)
        sc_ref[slot] = s_ctx
        m = jnp.maximum(s_ctx.max(axis=0, keepdims=True), sink)
        if win is not None:
            mask = jnp.concatenate([keep[kind] for kind in kinds], axis=0)
            s_win = jnp.where(mask, _dot_nt(k_ref[win, :], qs), -jnp.inf)
            sw_ref[slot, 0:s_win.shape[0], :] = s_win
            m = jnp.maximum(m, s_win.max(axis=0, keepdims=True))
        return m

    def weigh(slot, m, win, dst_ref, r0):
        p_ctx = jnp.exp2(sc_ref[slot] - m)
        den = p_ctx.sum(axis=0, keepdims=True) + jnp.exp2(sink - m)
        o = _dot(vct_ref[...], p_ctx.astype(BF16))
        if win is not None:
            p_win = jnp.exp2(sw_ref[slot, 0:win.stop - win.start, :] - m)
            den = den + p_win.sum(axis=0, keepdims=True)
            o = o + _dot(vt_ref[:, win], p_win.astype(BF16))
        o = o * (1.0 / den)
        o = jnp.where(upper, o[:, :W], o[:, W:])
        dst_ref[r0:r0 + W, :] = o.T.astype(dst_ref.dtype)

    work = []
    for n in range(n_blocks):
        first, last = max(n - 1, 0), min(n + 1, n_blocks - 1)
        kinds = (["prev"] if n > 0 else []) + ["cur"] + (["next"] if n < n_blocks - 1 else [])
        work.append((q_ref, slice(first * W, (last + 1) * W), kinds, o_ref, n * W))
    if need_ctx:
        for n in range(qc_ref.shape[0] // W):
            work.append((qc_ref, None, None, oc_ref, n * W))

    m_prev = None
    for i, (src_ref, win, kinds, dst_ref, r0) in enumerate(work):
        m = scores(i % 2, src_ref[r0:r0 + W, :], win, kinds)
        if i > 0:
            _, pwin, _, pdst, pr0 = work[i - 1]
            weigh((i - 1) % 2, m_prev, pwin, pdst, pr0)
        m_prev = m
    _, pwin, _, pdst, pr0 = work[-1]
    weigh((len(work) - 1) % 2, m_prev, pwin, pdst, pr0)


def _window_attn(pc, sink, batch, seq, ctx_len, need_ctx):
    ctx_k0 = batch * seq // ctx_len

    def lat(col):
        return pl.BlockSpec((seq, LANES), lambda b, h: (b, col + h))

    def ctx(col):
        return pl.BlockSpec((ctx_len, LANES), lambda b, h: (ctx_k0 + b, col + h))

    in_specs = [lat(0), lat(2), lat(4), ctx(2), ctx(4)]
    out_specs = [pl.BlockSpec((seq, LANES), lambda b, h: (b, h))]
    out_shape = [jax.ShapeDtypeStruct((batch * seq, 2 * LANES), BF16)]
    if need_ctx:
        in_specs.append(ctx(0))
        out_specs.append(pl.BlockSpec((ctx_len, LANES), lambda b, h: (b, h)))
        out_shape.append(jax.ShapeDtypeStruct((batch * ctx_len, 2 * LANES), BF16))
    args = [pc] * len(in_specs) + [sink]
    in_specs.append(pl.BlockSpec((1, 1, 2 * WINDOW), lambda b, h: (h, 0, 0)))
    out = pl.pallas_call(
        functools.partial(_window_attn_kernel, need_ctx=need_ctx),
        grid=(batch, 2),
        in_specs=in_specs,
        out_specs=out_specs,
        out_shape=out_shape,
        scratch_shapes=[pltpu.VMEM((LANES, seq), BF16), pltpu.VMEM((LANES, ctx_len), BF16),
                        pltpu.VMEM((2, 3 * WINDOW, 2 * WINDOW), F32),
                        pltpu.VMEM((2, ctx_len, 2 * WINDOW), F32)],
        compiler_params=_params("arbitrary", "arbitrary"),
        name="window_attn",
    )(*args)
    return out[0], (out[1] if need_ctx else None)


def _out_mlp_kernel(*refs, rows, has_ctx, final_norm):
    n_x = len(rows.arrays)
    n_mix = 6 if has_ctx else 3
    x_refs = refs[:n_x]
    mix_refs = refs[n_x:n_x + n_mix]
    mod_ref, g_ref, wo_ref, w1_ref, w2_ref = refs[n_x + n_mix:n_x + n_mix + 5]
    rest = refs[n_x + n_mix + 5:]
    if final_norm:
        gf_ref, o_ref = rest
    else:
        (o_ref,) = rest
    mod = mod_ref[0]
    is_lat = pl.program_id(0) < rows.n_lat_tiles
    mix, r0 = None, 0
    for i in range(3):
        a = mix_refs[i][...]
        if has_ctx:
            a = jnp.where(is_lat, a, mix_refs[3 + i][...])
        t = _dot(a, wo_ref[r0:r0 + a.shape[1], :])
        mix = t if mix is None else mix + t
        r0 += a.shape[1]
    x1 = rows.load(x_refs) + mod[2:3] * mix
    h = _modulated_norm(x1, g_ref[...], mod[3:4], mod[4:5]).astype(BF16)
    n_ff = w1_ref.shape[1]
    step = D_MODEL
    acc = None
    for c0 in range(0, n_ff, step):
        a = jnp.maximum(_dot(h, w1_ref[:, c0:c0 + step]), 0.0)
        t = _dot((a * a).astype(BF16), w2_ref[c0:c0 + step, :])
        acc = t if acc is None else acc + t
    x2 = x1 + mod[5:6] * acc
    if final_norm:
        ms = jnp.mean(x2 * x2, axis=-1, keepdims=True)
        x2 = x2 * lax.rsqrt(ms + NORM_EPS) * gf_ref[...]
    o_ref[...] = x2


def _out_mlp(rows, lat_mix, ctx_mix, mods, g, wo, w1, w2, layer, g_final, n_ctx_mod):
    d = rows.arrays[0].shape[1]
    n_lat_tiles, tile = rows.n_lat_tiles, rows.tile
    has_ctx = ctx_mix is not None
    final_norm = g_final is not None
    n_tiles = rows.n_tiles if has_ctx else n_lat_tiles

    in_specs = rows.specs()
    args = list(rows.arrays)
    for a in lat_mix:
        in_specs.append(pl.BlockSpec((tile, a.shape[1]), lambda i: (jnp.minimum(i, n_lat_tiles - 1), 0)))
        args.append(a)
    if has_ctx:
        for a in ctx_mix:
            in_specs.append(pl.BlockSpec((tile, a.shape[1]),
                                         lambda i: (jnp.maximum(i - n_lat_tiles, 0), 0)))
            args.append(a)
    in_specs += [rows.mod_spec(n_ctx_mod), _const_spec((1, d)),
                 _layer_spec(wo, layer), _layer_spec(w1, layer), _layer_spec(w2, layer)]
    args += [mods, g, wo, w1, w2]
    if final_norm:
        in_specs.append(_const_spec((1, d)))
        args.append(g_final)
    return pl.pallas_call(
        functools.partial(_out_mlp_kernel, rows=rows, has_ctx=has_ctx, final_norm=final_norm),
        grid=(n_tiles,),
        in_specs=in_specs,
        out_specs=pl.BlockSpec((tile, d), lambda i: (i, 0)),
        out_shape=jax.ShapeDtypeStruct((n_tiles * tile, d), F32),
        compiler_params=_params("arbitrary"),
        name="out_mlp",
    )(*args)


def _rope_tables(seq):
    rows = seq // GRID_W
    r = jnp.repeat(jnp.arange(rows, dtype=F32), GRID_W)
    col = jnp.tile(jnp.arange(GRID_W, dtype=F32), rows)
    nf = HEAD_DIM // 4
    inv = ROPE_BASE ** (-jnp.arange(nf, dtype=F32) / nf)
    ar, ac = r[:, None] * inv, col[:, None] * inv
    ang = jnp.concatenate([ar, ar, ac, ac], axis=-1)
    ang = jnp.concatenate([ang, ang], axis=-1)
    cos, sin = jnp.cos(ang), jnp.sin(ang)
    up = (jnp.arange(LANES) // nf) % 2 == 0
    sin_up = jnp.where(up, -sin, 0.0)
    sin_dn = jnp.where(up, 0.0, sin)
    ident = jnp.zeros((IN_TILE, LANES), F32)
    return (jnp.concatenate([cos, ident + 1.0]), jnp.concatenate([sin_up, ident]),
            jnp.concatenate([sin_dn, ident]))


def kernel(x, c, ctx, c_ctx, w_ada, b_ada, g_mix, g_mlp, w_in, w_out, lam_q1, lam_k1, lam_q2, lam_k2,
           subln_g, ret_decay_fwd, ret_decay_bwd, sink_logit, w_mlp1, w_mlp2, g_final):
    batch, seq, d = x.shape
    ctx_len = ctx.shape[1]
    depth = w_ada.shape[0]
    assert d == D_MODEL and seq % IN_TILE == 0 and (batch * ctx_len) % IN_TILE == 0
    assert IN_TILE % ROW_TILE == 0
    assert ctx_len == Q_SUB and ctx_len % WINDOW == 0 and seq % GRID_W == 0
    n_lat = batch * seq

    mod_rows = 16
    cc = jnp.concatenate([c, c_ctx[None], jnp.zeros((mod_rows - batch - 1, d), F32)], axis=0)
    mods = _ada_params(cc, w_ada, b_ada).reshape(depth, mod_rows, 6, d)
    cos, sup, sdn = _rope_tables(seq)
    w_in, w_out, w_mlp1, w_mlp2 = (w.astype(BF16) for w in (w_in, w_out, w_mlp1, w_mlp2))

    stream = (x.reshape(n_lat, d), ctx.reshape(batch * ctx_len, d))
    for l in range(depth):
        need_ctx = l < depth - 1
        last = l == depth - 1
        lam_init = 0.8 - 0.6 * math.exp(-0.3 * l)
        pa, pb, pc = _in_proj(_Rows(stream, n_lat, seq, IN_TILE), mods[l], g_mix[l][None], w_in, l,
                              cos, sup, sdn, batch)
        lamv = jnp.stack([lam_q1[l], lam_k1[l], lam_q2[l], lam_k2[l]])
        oa, oa_c = _diff_attn(pa, lamv, subln_g[l][:, None], lam_init, batch, seq, ctx_len, need_ctx)
        dec_f = jnp.repeat(ret_decay_fwd[l], HEAD_DIM).reshape(2, 1, LANES)
        dec_b = jnp.repeat(ret_decay_bwd[l], HEAD_DIM).reshape(2, 1, LANES)
        ob, ob_c = _retention(pb, dec_f, dec_b, batch, seq, ctx_len, need_ctx)
        sink = jnp.repeat(sink_logit[l], WINDOW).reshape(2, 1, 2 * WINDOW)
        oc, oc_c = _window_attn(pc, sink, batch, seq, ctx_len, need_ctx)
        xs = _out_mlp(_Rows(stream, n_lat, seq, ROW_TILE), (oa, ob, oc),
                      (oa_c, ob_c, oc_c) if need_ctx else None, mods[l], g_mlp[l][None],
                      w_out, w_mlp1, w_mlp2, l, g_final[None] if last else None, batch)
        stream = (xs,)
    return xs.reshape(batch, seq, d)
```
